```python
import math
import jax, jax.numpy as jnp
from jax import lax
import numpy as np

D_MODEL = 1024
BATCH = 4
SEQ = 4096
DEPTH = 2

PLE_DIM = 256
N_NORMS = 6
EPS = 1e-6
D_RNN = 1024
N_RNN_BLOCKS = 16
RNN_BLOCK = D_RNN // N_RNN_BLOCKS
CONV_WIDTH = 4
LRU_C = 8.0
N_DIFF_HEADS = 8
DIFF_HEAD_DIM = 64
DIFF_V_DIM = 2 * DIFF_HEAD_DIM
D_QK = N_DIFF_HEADS * 2 * DIFF_HEAD_DIM
D_ATTN = N_DIFF_HEADS * DIFF_V_DIM
ROPE_THETA = 10000.0
Q_BLOCK = 128
D_FF = 3584
N_EXPERTS = 8
TOP_K = 2
N_DENSE_LAYERS = (DEPTH + 1) // 2
N_MOE_LAYERS = DEPTH // 2
D_IN = 2 * D_RNN + 2 * D_QK + D_ATTN + 2 * D_MODEL
SPLIT_POINTS = (D_RNN, 2 * D_RNN, 2 * D_RNN + D_QK, 2 * D_RNN + 2 * D_QK,
                2 * D_RNN + 2 * D_QK + D_ATTN, 2 * D_RNN + 2 * D_QK + D_ATTN + D_MODEL)

kernel_name = 'hybrid_rglru_diffattn_moe_block'


def rmsnorm(x, g):
    xf = x.astype(jnp.float32)
    xf = xf * lax.rsqrt(jnp.mean(xf * xf, axis=-1, keepdims=True) + EPS)
    return (xf * g.astype(jnp.float32)).astype(x.dtype)


def rope_tables(positions):
    inv_freq = ROPE_THETA ** (-jnp.arange(0, DIFF_HEAD_DIM, 2, dtype=jnp.float32) / DIFF_HEAD_DIM)
    ang = positions.astype(jnp.float32)[..., None] * inv_freq
    ang = jnp.concatenate([ang, ang], axis=-1)
    return jnp.cos(ang), jnp.sin(ang)


def apply_rope(t, cos, sin):
    half = DIFF_HEAD_DIM // 2
    tf = t.astype(jnp.float32)
    rot = jnp.concatenate([-tf[..., half:], tf[..., :half]], axis=-1)
    c = cos[:, :, None, None, :]
    s = sin[:, :, None, None, :]
    return (tf * c + rot * s).astype(t.dtype)


def causal_depthwise_conv(u, w, b):
    y = lax.conv_general_dilated(u, w[:, None, :].astype(u.dtype), window_strides=(1,),
                                 padding=[(CONV_WIDTH - 1, 0)],
                                 dimension_numbers=('NWC', 'WIO', 'NWC'),
                                 feature_group_count=D_RNN)
    return y + b


def rg_lru(u, w_a, b_a, w_x, b_x, lam):
    bsz, seq, _ = u.shape
    ub = u.reshape(bsz, seq, N_RNN_BLOCKS, RNN_BLOCK)
    r = jax.nn.sigmoid(jnp.einsum('bsnc,ncd->bsnd', ub, w_a).reshape(bsz, seq, D_RNN) + b_a)
    i_gate = jax.nn.sigmoid(jnp.einsum('bsnc,ncd->bsnd', ub, w_x).reshape(bsz, seq, D_RNN) + b_x)
    log_a = -LRU_C * r.astype(jnp.float32) * jax.nn.softplus(-lam.astype(jnp.float32))
    a = jnp.exp(log_a)
    beta = jnp.sqrt(-jnp.expm1(2.0 * log_a))
    b = beta * (i_gate * u).astype(jnp.float32)

    def combine(left, right):
        a_l, b_l = left
        a_r, b_r = right
        return a_l * a_r, a_r * b_l + b_r

    _, h = lax.associative_scan(combine, (a, b), axis=1)
    return h.astype(u.dtype)


def diff_attention(q, k, v, lam, lambda_init, g_sub):
    seq = q.shape[2]
    scale = DIFF_HEAD_DIM ** -0.5
    outs = []
    for j in range(seq // Q_BLOCK):
        q0 = j * Q_BLOCK
        kv_len = q0 + Q_BLOCK
        s = jnp.einsum('bhqmd,bhkmd->bhmqk', q[:, :, q0:kv_len], k[:, :, :kv_len]).astype(jnp.float32) * scale
        causal = (q0 + jnp.arange(Q_BLOCK))[:, None] >= jnp.arange(kv_len)[None, :]
        s = jnp.where(causal, s, -1e30)
        prob = jax.nn.softmax(s, axis=-1)
        w = prob[:, :, 0] - lam * prob[:, :, 1]
        outs.append(jnp.einsum('bhqk,bhkd->bhqd', w.astype(v.dtype), v[:, :, :kv_len]))
    o = jnp.concatenate(outs, axis=2)
    o = rmsnorm(o, g_sub) * (1.0 - lambda_init)
    bsz = o.shape[0]
    return o.transpose(0, 2, 1, 3).reshape(bsz, seq, D_ATTN)


def swiglu(h, w_gate, w_up, w_down):
    return (jax.nn.silu(h @ w_gate) * (h @ w_up)) @ w_down


def moe_swiglu(h, router_w, w_gate, w_up, w_down):
    bsz, seq, d = h.shape
    t = h.reshape(bsz * seq, d)
    logits = (t @ router_w).astype(jnp.float32)
    top_val, top_idx = lax.top_k(logits, TOP_K)
    top_w = jax.nn.softmax(top_val, axis=-1)
    combine = jnp.sum(jax.nn.one_hot(top_idx, N_EXPERTS, dtype=jnp.float32) * top_w[..., None], axis=1)
    out = jnp.zeros_like(t)
    for e in range(N_EXPERTS):
        out = out + combine[:, e:e + 1].astype(t.dtype) * swiglu(t, w_gate[e], w_up[e], w_down[e])
    return out.reshape(bsz, seq, d)


def setup_inputs(seed: int = 0) -> dict:
    key = jax.random.key(seed)
    ks = jax.random.split(key, 28)
    f32 = jnp.float32

    def nrm(k, shape, scale):
        return jax.random.normal(k, shape, f32) * scale

    x = nrm(ks[0], (BATCH, SEQ, D_MODEL), 1.0)
    p = nrm(ks[1], (DEPTH, BATCH, SEQ, PLE_DIM), 1.0)
    offsets = jax.random.randint(ks[2], (BATCH, 1), 0, 1024, dtype=jnp.int32)
    positions = offsets + jnp.arange(SEQ, dtype=jnp.int32)[None, :]
    norm_g = 1.0 + nrm(ks[3], (DEPTH, N_NORMS, D_MODEL), 0.01)
    w_in = nrm(ks[4], (DEPTH, D_MODEL, D_IN), D_MODEL ** -0.5)
    conv_w = nrm(ks[5], (DEPTH, CONV_WIDTH, D_RNN), CONV_WIDTH ** -0.5)
    conv_b = nrm(ks[6], (DEPTH, D_RNN), 0.01)
    lru_wa = nrm(ks[7], (DEPTH, N_RNN_BLOCKS, RNN_BLOCK, RNN_BLOCK), RNN_BLOCK ** -0.5)
    lru_ba = nrm(ks[8], (DEPTH, D_RNN), 0.01)
    lru_wx = nrm(ks[9], (DEPTH, N_RNN_BLOCKS, RNN_BLOCK, RNN_BLOCK), RNN_BLOCK ** -0.5)
    lru_bx = nrm(ks[10], (DEPTH, D_RNN), 0.01)
    a0 = jax.random.uniform(ks[11], (DEPTH, D_RNN), f32, 0.9, 0.999)
    a_root = a0 ** (1.0 / LRU_C)
    lru_lam = jnp.log(a_root) - jnp.log1p(-a_root)
    diff_lambda = nrm(ks[12], (DEPTH, 4, DIFF_HEAD_DIM), 0.1)
    subln_g = 1.0 + nrm(ks[13], (DEPTH, DIFF_V_DIM), 0.01)
    w_proj_a = nrm(ks[14], (DEPTH, D_RNN, D_MODEL), D_RNN ** -0.5)
    w_proj_b = nrm(ks[15], (DEPTH, D_ATTN, D_MODEL), D_ATTN ** -0.5)
    w_out = nrm(ks[16], (DEPTH, D_MODEL, D_MODEL), D_MODEL ** -0.5)
    ffn_w_gate = nrm(ks[17], (N_DENSE_LAYERS, D_MODEL, D_FF), D_MODEL ** -0.5)
    ffn_w_up = nrm(ks[18], (N_DENSE_LAYERS, D_MODEL, D_FF), D_MODEL ** -0.5)
    ffn_w_down = nrm(ks[19], (N_DENSE_LAYERS, D_FF, D_MODEL), D_FF ** -0.5)
    router_w = nrm(ks[20], (N_MOE_LAYERS, D_MODEL, N_EXPERTS), D_MODEL ** -0.5)
    moe_w_gate = nrm(ks[21], (N_MOE_LAYERS, N_EXPERTS, D_MODEL, D_FF), D_MODEL ** -0.5)
    moe_w_up = nrm(ks[22], (N_MOE_LAYERS, N_EXPERTS, D_MODEL, D_FF), D_MODEL ** -0.5)
    moe_w_down = nrm(ks[23], (N_MOE_LAYERS, N_EXPERTS, D_FF, D_MODEL), D_FF ** -0.5)
    ple_w_proj = nrm(ks[24], (DEPTH, PLE_DIM, D_MODEL), PLE_DIM ** -0.5)
    ple_w_gate = nrm(ks[25], (DEPTH, D_MODEL, D_MODEL), D_MODEL ** -0.5)
    return {'x': x, 'p': p, 'positions': positions, 'norm_g': norm_g, 'w_in': w_in,
            'conv_w': conv_w, 'conv_b': conv_b, 'lru_wa': lru_wa, 'lru_ba': lru_ba,
            'lru_wx': lru_wx, 'lru_bx': lru_bx, 'lru_lam': lru_lam,
            'diff_lambda': diff_lambda, 'subln_g': subln_g, 'w_proj_a': w_proj_a,
            'w_proj_b': w_proj_b, 'w_out': w_out, 'ffn_w_gate': ffn_w_gate,
            'ffn_w_up': ffn_w_up, 'ffn_w_down': ffn_w_down, 'router_w': router_w,
            'moe_w_gate': moe_w_gate, 'moe_w_up': moe_w_up, 'moe_w_down': moe_w_down,
            'ple_w_proj': ple_w_proj, 'ple_w_gate': ple_w_gate}


def reference(x, p, positions, norm_g, w_in, conv_w, conv_b, lru_wa, lru_ba, lru_wx, lru_bx,
              lru_lam, diff_lambda, subln_g, w_proj_a, w_proj_b, w_out, ffn_w_gate, ffn_w_up,
              ffn_w_down, router_w, moe_w_gate, moe_w_up, moe_w_down, ple_w_proj, ple_w_gate):
    bsz, seq, _ = x.shape
    cos, sin = rope_tables(positions)
    for i in range(DEPTH):
        lambda_init = 0.8 - 0.6 * math.exp(-0.3 * i)
        h = rmsnorm(x, norm_g[i, 0])
        proj = h @ w_in[i]
        rnn_gate, rnn_in, q, k, v, g_a, g_b = jnp.split(proj, SPLIT_POINTS, axis=-1)
        u = causal_depthwise_conv(rnn_in, conv_w[i], conv_b[i])
        y_a = rg_lru(u, lru_wa[i], lru_ba[i], lru_wx[i], lru_bx[i], lru_lam[i]) * jax.nn.gelu(rnn_gate)
        q = apply_rope(q.reshape(bsz, seq, N_DIFF_HEADS, 2, DIFF_HEAD_DIM), cos, sin).transpose(0, 2, 1, 3, 4)
        k = apply_rope(k.reshape(bsz, seq, N_DIFF_HEADS, 2, DIFF_HEAD_DIM), cos, sin).transpose(0, 2, 1, 3, 4)
        v = v.reshape(bsz, seq, N_DIFF_HEADS, DIFF_V_DIM).transpose(0, 2, 1, 3)
        lq1, lk1, lq2, lk2 = diff_lambda[i].astype(jnp.float32)
        lam = jnp.exp(jnp.sum(lq1 * lk1)) - jnp.exp(jnp.sum(lq2 * lk2)) + lambda_init
        y_b = diff_attention(q, k, v, lam, lambda_init, subln_g[i])
        merged = jax.nn.sigmoid(g_a) * (y_a @ w_proj_a[i]) + jax.nn.sigmoid(g_b) * (y_b @ w_proj_b[i])
        x = x + rmsnorm(merged @ w_out[i], norm_g[i, 1])
        h = rmsnorm(x, norm_g[i, 2])
        if i % 2 == 0:
            f = swiglu(h, ffn_w_gate[i // 2], ffn_w_up[i // 2], ffn_w_down[i // 2])
        else:
            f = moe_swiglu(h, router_w[i // 2], moe_w_gate[i // 2], moe_w_up[i // 2], moe_w_down[i // 2])
        x = x + rmsnorm(f, norm_g[i, 3])
        gate = jax.nn.sigmoid(rmsnorm(x, norm_g[i, 4]) @ ple_w_gate[i])
        x = x + rmsnorm(gate * (p[i] @ ple_w_proj[i]), norm_g[i, 5])
    return x
```

```python
import functools
import math

import jax
import jax.numpy as jnp
from jax import lax
from jax.experimental import pallas as pl
from jax.experimental.pallas import tpu as pltpu

EPS = 1e-6
CONV_WIDTH = 4
LRU_C = 8.0
N_RNN_BLOCKS = 16
N_HEADS = 8
HEAD_DIM = 64
V_DIM = 2 * HEAD_DIM
ROPE_THETA = 10000.0
TOP_K = 2
LANES = 128
SUBLANES = 8
MXU_DIM = 256
VMEM_LIMIT = 56 * 1024 * 1024
NEG_BIG = -1e30

f32 = jnp.float32
bf16 = jnp.bfloat16


def _cparams(*sem):
    return pltpu.CompilerParams(dimension_semantics=sem, vmem_limit_bytes=VMEM_LIMIT)


def _rms(x, g):
    return x * lax.rsqrt(jnp.mean(x * x, axis=-1, keepdims=True) + EPS) * g


def _sigmoid(x):
    return 1.0 / (1.0 + jnp.exp(-x))


def _rope_tab_kernel(pos_ref, inv_ref, cos_ref, sin_ref):
    ang = pos_ref[...] * inv_ref[...]
    cos_ref[...] = jnp.cos(ang)
    sin_ref[...] = jnp.sin(ang)


def rope_tables(pos_f, inv_row, tm):
    t = pos_f.shape[0]
    return pl.pallas_call(
        _rope_tab_kernel,
        grid=(t // tm,),
        in_specs=[pl.BlockSpec((tm, 1), lambda i: (i, 0)),
                  pl.BlockSpec((1, LANES), lambda i: (0, 0))],
        out_specs=[pl.BlockSpec((tm, LANES), lambda i: (i, 0))] * 2,
        out_shape=[jax.ShapeDtypeStruct((t, LANES), f32)] * 2,
        compiler_params=_cparams("parallel"),
        name="rope_tables",
    )(pos_f, inv_row)


def _norm_kernel(x_ref, g_ref, o_ref):
    o_ref[...] = _rms(x_ref[...], g_ref[...]).astype(o_ref.dtype)


def norm_cast(x, g, tm):
    t, d = x.shape
    return pl.pallas_call(
        _norm_kernel,
        grid=(t // tm,),
        in_specs=[pl.BlockSpec((tm, d), lambda i: (i, 0)),
                  pl.BlockSpec((1, d), lambda i: (0, 0))],
        out_specs=pl.BlockSpec((tm, d), lambda i: (i, 0)),
        out_shape=jax.ShapeDtypeStruct((t, d), bf16),
        compiler_params=_cparams("parallel"),
        name="norm_cast",
    )(x, g)


def _in_proj_kernel(h_ref, w_ref, cos_ref, sin_ref, o_ref, *, q_tile, k_tile):
    j = pl.program_id(1)
    acc = jnp.dot(h_ref[...], w_ref[...], preferred_element_type=f32)
    is_rope = jnp.logical_or(j == q_tile, j == k_tile)

    @pl.when(jnp.logical_not(is_rope))
    def _():
        o_ref[...] = acc.astype(o_ref.dtype)

    @pl.when(is_rope)
    def _():
        scale = jnp.where(j == q_tile, HEAD_DIM ** -0.5, 1.0).astype(f32)
        cos = cos_ref[...] * scale
        sin = sin_ref[...] * scale
        lane = lax.broadcasted_iota(jnp.int32, cos.shape, 1)
        first_half = (lane % HEAD_DIM) < (HEAD_DIM // 2)
        for g in range(acc.shape[1] // LANES):
            t = acc[:, g * LANES:(g + 1) * LANES]
            up = pltpu.roll(t, HEAD_DIM // 2, axis=1)
            down = pltpu.roll(t, LANES - HEAD_DIM // 2, axis=1)
            rot = jnp.where(first_half, -down, up)
            o_ref[:, g * LANES:(g + 1) * LANES] = (t * cos + rot * sin).astype(o_ref.dtype)


def in_proj(h, w, cos, sin, tm, tn):
    t, d = h.shape
    n = w.shape[1]
    kern = functools.partial(_in_proj_kernel, q_tile=2 * d // tn, k_tile=3 * d // tn)
    return pl.pallas_call(
        kern,
        grid=(t // tm, n // tn),
        in_specs=[pl.BlockSpec((tm, d), lambda i, j: (i, 0)),
                  pl.BlockSpec((d, tn), lambda i, j: (0, j)),
                  pl.BlockSpec((tm, LANES), lambda i, j: (i, 0)),
                  pl.BlockSpec((tm, LANES), lambda i, j: (i, 0))],
        out_specs=pl.BlockSpec((tm, tn), lambda i, j: (i, j)),
        out_shape=jax.ShapeDtypeStruct((t, n), bf16),
        compiler_params=_cparams("parallel", "arbitrary"),
        name="in_proj",
    )(h, w, cos, sin)


def _rglru_kernel(gate_ref, x_ref, cw_ref, cb_ref, wbd_ref, ba_ref, bx_ref, lam_ref, o_ref,
                  xbuf, a_s, b_s, hcar, *, tc):
    c = pl.program_id(1)
    d = x_ref.shape[1]

    @pl.when(c == 0)
    def _():
        xbuf[0:SUBLANES, :] = jnp.zeros((SUBLANES, d), f32)
        hcar[...] = jnp.zeros_like(hcar)

    @pl.when(c > 0)
    def _():
        xbuf[0:SUBLANES, :] = xbuf[tc:tc + SUBLANES, :]

    xbuf[SUBLANES:SUBLANES + tc, :] = x_ref[...].astype(f32)

    u = cb_ref[...] + cw_ref[CONV_WIDTH - 1:CONV_WIDTH, :] * xbuf[SUBLANES:SUBLANES + tc, :]
    for k in range(CONV_WIDTH - 1):
        off = SUBLANES - (CONV_WIDTH - 1) + k
        u = u + cw_ref[k:k + 1, :] * xbuf[off:off + tc, :]

    ub = u.astype(bf16)
    z = -lam_ref[...]
    softplus = jnp.maximum(z, 0.0) + jnp.log(1.0 + jnp.exp(-jnp.abs(z)))
    for g in range(d // MXU_DIM):
        sl = slice(g * MXU_DIM, (g + 1) * MXU_DIM)
        zz = jnp.dot(ub[:, sl], wbd_ref[g], preferred_element_type=f32)
        r = _sigmoid(zz[:, :MXU_DIM] + ba_ref[:, sl])
        i_gate = _sigmoid(zz[:, MXU_DIM:] + bx_ref[:, sl])
        log_a = (-LRU_C) * r * softplus[:, sl]
        a = jnp.exp(log_a)
        beta = jnp.sqrt(1.0 - a * a)
        a_s[:, sl] = a
        b_s[:, sl] = beta * (i_gate * u[:, sl])

    def body(t, h):
        h = a_s[pl.ds(t, 1), :] * h + b_s[pl.ds(t, 1), :]
        b_s[pl.ds(t, 1), :] = h
        return h

    hcar[...] = lax.fori_loop(0, tc, body, hcar[...], unroll=8)

    gt = gate_ref[...].astype(f32)
    gelu = 0.5 * gt * (1.0 + jnp.tanh(math.sqrt(2.0 / math.pi) * (gt + 0.044715 * (gt * gt * gt))))
    o_ref[...] = (b_s[...] * gelu).astype(o_ref.dtype)


def rglru(proj, cw, cb, wbd, ba, bx, lam, batch, tc):
    t = proj.shape[0]
    d = cw.shape[1]
    nc = t // batch // tc
    row = lambda b, c: (b * nc + c, 0)
    const2 = lambda b, c: (0, 0)
    return pl.pallas_call(
        functools.partial(_rglru_kernel, tc=tc),
        grid=(batch, nc),
        in_specs=[pl.BlockSpec((tc, d), lambda b, c: (b * nc + c, 0)),
                  pl.BlockSpec((tc, d), lambda b, c: (b * nc + c, 1)),
                  pl.BlockSpec((CONV_WIDTH, d), const2),
                  pl.BlockSpec((1, d), const2),
                  pl.BlockSpec(wbd.shape, lambda b, c: (0, 0, 0)),
                  pl.BlockSpec((1, d), const2),
                  pl.BlockSpec((1, d), const2),
                  pl.BlockSpec((1, d), const2)],
        out_specs=pl.BlockSpec((tc, d), row),
        out_shape=jax.ShapeDtypeStruct((t, d), bf16),
        scratch_shapes=[pltpu.VMEM((tc + SUBLANES, d), f32),
                        pltpu.VMEM((tc, d), f32),
                        pltpu.VMEM((tc, d), f32),
                        pltpu.VMEM((1, d), f32)],
        compiler_params=_cparams("parallel", "arbitrary"),
        name="rglru",
    )(proj, proj, cw, cb, wbd, ba, bx, lam)


def _diff_attn_kernel(q_ref, k_ref, v_ref, dl_ref, gs_ref, o_ref,
                      m1, l1, acc1, m2, l2, acc2, *, tq, lambda_init):
    qi = pl.program_id(2)
    q = q_ref[...]
    lane = lax.broadcasted_iota(jnp.int32, q.shape, 1)
    zero = jnp.zeros_like(q)
    qa = jnp.where(lane < HEAD_DIM, q, zero)
    qb = jnp.where(lane >= HEAD_DIM, q, zero)

    for m, l, acc in ((m1, l1, acc1), (m2, l2, acc2)):
        m[...] = jnp.full_like(m, NEG_BIG)
        l[...] = jnp.zeros_like(l)
        acc[...] = jnp.zeros_like(acc)

    nt = (((1,), (1,)), ((), ()))

    def step(j, masked):
        start = pl.multiple_of(j * tq, tq)
        k = k_ref[pl.ds(start, tq), :]
        v = v_ref[pl.ds(start, tq), :]
        if masked:
            row = lax.broadcasted_iota(jnp.int32, (tq, tq), 0)
            col = lax.broadcasted_iota(jnp.int32, (tq, tq), 1)
            causal = row >= col
        for qm, m, l, acc in ((qa, m1, l1, acc1), (qb, m2, l2, acc2)):
            s = lax.dot_general(qm, k, nt, preferred_element_type=f32)
            if masked:
                s = jnp.where(causal, s, NEG_BIG)
            m_old = m[...]
            m_new = jnp.maximum(m_old, jnp.max(s, axis=-1, keepdims=True))
            alpha = jnp.exp(m_old - m_new)
            p = jnp.exp(s - m_new)
            l[...] = alpha * l[...] + jnp.sum(p, axis=-1, keepdims=True)
            acc[...] = alpha * acc[...] + jnp.dot(p.astype(bf16), v, preferred_element_type=f32)
            m[...] = m_new

    def body(j, carry):
        step(j, False)
        return carry

    lax.fori_loop(0, qi, body, 0)
    step(qi, True)

    dl = dl_ref[...]
    lam = (jnp.exp(jnp.sum(dl[0:1, :] * dl[1:2, :], axis=-1, keepdims=True))
           - jnp.exp(jnp.sum(dl[2:3, :] * dl[3:4, :], axis=-1, keepdims=True)) + lambda_init)
    o = acc1[...] / l1[...] - lam * (acc2[...] / l2[...])
    o_ref[...] = (_rms(o, gs_ref[...]) * (1.0 - lambda_init)).astype(o_ref.dtype)


def diff_attn(proj, dl, gs, batch, tq, lambda_init):
    t = proj.shape[0]
    seq = t // batch
    nq = seq // tq
    d = N_HEADS * V_DIM
    q_col, k_col, v_col = 2 * N_HEADS, 3 * N_HEADS, 4 * N_HEADS
    kern = functools.partial(_diff_attn_kernel, tq=tq, lambda_init=lambda_init)
    return pl.pallas_call(
        kern,
        grid=(batch, N_HEADS, nq),
        in_specs=[pl.BlockSpec((tq, V_DIM), lambda b, h, i: (b * nq + i, q_col + h)),
                  pl.BlockSpec((seq, V_DIM), lambda b, h, i: (b, k_col + h)),
                  pl.BlockSpec((seq, V_DIM), lambda b, h, i: (b, v_col + h)),
                  pl.BlockSpec((4, HEAD_DIM), lambda b, h, i: (0, 0)),
                  pl.BlockSpec((1, V_DIM), lambda b, h, i: (0, 0))],
        out_specs=pl.BlockSpec((tq, V_DIM), lambda b, h, i: (b * nq + i, h)),
        out_shape=jax.ShapeDtypeStruct((t, d), bf16),
        scratch_shapes=[pltpu.VMEM((tq, 1), f32), pltpu.VMEM((tq, 1), f32), pltpu.VMEM((tq, V_DIM), f32),
                        pltpu.VMEM((tq, 1), f32), pltpu.VMEM((tq, 1), f32), pltpu.VMEM((tq, V_DIM), f32)],
        compiler_params=_cparams("parallel", "parallel", "arbitrary"),
        name="diff_attn",
    )(proj, proj, proj, dl, gs)


def _merge_kernel(ya_ref, yb_ref, ga_ref, gb_ref, x_ref, wa_ref, wb_ref, wo_ref, g1_ref, g2_ref,
                  xo_ref, ho_ref):
    pa = jnp.dot(ya_ref[...], wa_ref[...], preferred_element_type=f32)
    pb = jnp.dot(yb_ref[...], wb_ref[...], preferred_element_type=f32)
    merged = _sigmoid(ga_ref[...].astype(f32)) * pa + _sigmoid(gb_ref[...].astype(f32)) * pb
    o = jnp.dot(merged.astype(bf16), wo_ref[...], preferred_element_type=f32)
    x_new = x_ref[...] + _rms(o, g1_ref[...])
    xo_ref[...] = x_new
    ho_ref[...] = _rms(x_new, g2_ref[...]).astype(ho_ref.dtype)


def merge(ya, yb, proj, x, wa, wb, wo, g1, g2, tm):
    t, d = x.shape
    ga_col = proj.shape[1] // d - 2
    row = lambda i: (i, 0)
    const = lambda i: (0, 0)
    return pl.pallas_call(
        _merge_kernel,
        grid=(t // tm,),
        in_specs=[pl.BlockSpec((tm, d), row), pl.BlockSpec((tm, d), row),
                  pl.BlockSpec((tm, d), lambda i: (i, ga_col)),
                  pl.BlockSpec((tm, d), lambda i: (i, ga_col + 1)),
                  pl.BlockSpec((tm, d), row),
                  pl.BlockSpec((d, d), const), pl.BlockSpec((d, d), const), pl.BlockSpec((d, d), const),
                  pl.BlockSpec((1, d), const), pl.BlockSpec((1, d), const)],
        out_specs=[pl.BlockSpec((tm, d), row), pl.BlockSpec((tm, d), row)],
        out_shape=[jax.ShapeDtypeStruct((t, d), f32), jax.ShapeDtypeStruct((t, d), bf16)],
        compiler_params=_cparams("parallel"),
        name="merge",
    )(ya, yb, proj, proj, x, wa, wb, wo, g1, g2)


def _ffn_kernel(h_ref, x_ref, wg_ref, wu_ref, wd_ref, g3_ref, g4_ref, xo_ref, ho_ref, acc):
    f = pl.program_id(1)
    h = h_ref[...]
    gate = jnp.dot(h, wg_ref[...], preferred_element_type=f32)
    up = jnp.dot(h, wu_ref[...], preferred_element_type=f32)
    act = (gate * _sigmoid(gate) * up).astype(bf16)
    part = jnp.dot(act, wd_ref[...], preferred_element_type=f32)

    @pl.when(f == 0)
    def _():
        acc[...] = part

    @pl.when(f > 0)
    def _():
        acc[...] += part

    @pl.when(f == pl.num_programs(1) - 1)
    def _():
        x_new = x_ref[...] + _rms(acc[...], g3_ref[...])
        xo_ref[...] = x_new
        ho_ref[...] = _rms(x_new, g4_ref[...]).astype(ho_ref.dtype)


def ffn(h, x, wg, wu, wd, g3, g4, tm, tf):
    t, d = x.shape
    dff = wg.shape[1]
    row = lambda i, f: (i, 0)
    const = lambda i, f: (0, 0)
    return pl.pallas_call(
        _ffn_kernel,
        grid=(t // tm, dff // tf),
        in_specs=[pl.BlockSpec((tm, d), row), pl.BlockSpec((tm, d), row),
                  pl.BlockSpec((d, tf), lambda i, f: (0, f)),
                  pl.BlockSpec((d, tf), lambda i, f: (0, f)),
                  pl.BlockSpec((tf, d), lambda i, f: (f, 0)),
                  pl.BlockSpec((1, d), const), pl.BlockSpec((1, d), const)],
        out_specs=[pl.BlockSpec((tm, d), row), pl.BlockSpec((tm, d), row)],
        out_shape=[jax.ShapeDtypeStruct((t, d), f32), jax.ShapeDtypeStruct((t, d), bf16)],
        scratch_shapes=[pltpu.VMEM((tm, d), f32)],
        compiler_params=_cparams("parallel", "arbitrary"),
        name="ffn",
    )(h, x, wg, wu, wd, g3, g4)


def _router_kernel(h_ref, rw_ref, c_ref):
    logits = lax.dot_general(rw_ref[...], h_ref[...], (((1,), (1,)), ((), ())),
                             preferred_element_type=f32)
    e_idx = lax.broadcasted_iota(jnp.int32, logits.shape, 0)
    n_e = logits.shape[0]
    v1 = jnp.max(logits, axis=0, keepdims=True)
    i1 = jnp.min(jnp.where(logits == v1, e_idx, n_e), axis=0, keepdims=True)
    rest = jnp.where(e_idx == i1, -jnp.inf, logits)
    v2 = jnp.max(rest, axis=0, keepdims=True)
    i2 = jnp.min(jnp.where(rest == v2, e_idx, n_e), axis=0, keepdims=True)
    ex = jnp.exp(v2 - v1)
    w1 = 1.0 / (1.0 + ex)
    w2 = ex / (1.0 + ex)
    c_ref[...] = jnp.where(e_idx == i1, w1, 0.0) + jnp.where(e_idx == i2, w2, 0.0)


def router(h, rw_t, tm):
    t, d = h.shape
    n_e = rw_t.shape[0]
    return pl.pallas_call(
        _router_kernel,
        grid=(t // tm,),
        in_specs=[pl.BlockSpec((tm, d), lambda i: (i, 0)),
                  pl.BlockSpec((n_e, d), lambda i: (0, 0))],
        out_specs=pl.BlockSpec((n_e, tm), lambda i: (0, i)),
        out_shape=jax.ShapeDtypeStruct((n_e, t), f32),
        compiler_params=_cparams("parallel"),
        name="router",
    )(h, rw_t)


def _moe_kernel(h_ref, x_ref, c_ref, wg_ref, wu_ref, wd_ref, g3_ref, g4_ref, xo_ref, ho_ref, acc):
    e = pl.program_id(1)
    f = pl.program_id(2)
    h = h_ref[...]
    gate = jnp.dot(h, wg_ref[0], preferred_element_type=f32)
    up = jnp.dot(h, wu_ref[0], preferred_element_type=f32)
    act = (gate * _sigmoid(gate) * up).astype(bf16)
    part = jnp.dot(act, wd_ref[0], preferred_element_type=f32)
    lane = lax.broadcasted_iota(jnp.int32, c_ref.shape, 1)
    w = jnp.sum(jnp.where(lane == e, c_ref[...], 0.0), axis=-1, keepdims=True)
    first = jnp.logical_and(e == 0, f == 0)

    @pl.when(first)
    def _():
        acc[...] = w * part

    @pl.when(jnp.logical_not(first))
    def _():
        acc[...] += w * part

    @pl.when(jnp.logical_and(e == pl.num_programs(1) - 1, f == pl.num_programs(2) - 1))
    def _():
        x_new = x_ref[...] + _rms(acc[...], g3_ref[...])
        xo_ref[...] = x_new
        ho_ref[...] = _rms(x_new, g4_ref[...]).astype(ho_ref.dtype)


def moe_dense(h, x, comb, wg, wu, wd, g3, g4, tm, tf):
    t, d = x.shape
    n_e, _, dff = wg.shape
    row = lambda i, e, f: (i, 0)
    const = lambda i, e, f: (0, 0)
    return pl.pallas_call(
        _moe_kernel,
        grid=(t // tm, n_e, dff // tf),
        in_specs=[pl.BlockSpec((tm, d), row), pl.BlockSpec((tm, d), row),
                  pl.BlockSpec((tm, n_e), row),
                  pl.BlockSpec((1, d, tf), lambda i, e, f: (e, 0, f)),
                  pl.BlockSpec((1, d, tf), lambda i, e, f: (e, 0, f)),
                  pl.BlockSpec((1, tf, d), lambda i, e, f: (e, f, 0)),
                  pl.BlockSpec((1, d), const), pl.BlockSpec((1, d), const)],
        out_specs=[pl.BlockSpec((tm, d), row), pl.BlockSpec((tm, d), row)],
        out_shape=[jax.ShapeDtypeStruct((t, d), f32), jax.ShapeDtypeStruct((t, d), bf16)],
        scratch_shapes=[pltpu.VMEM((tm, d), f32)],
        compiler_params=_cparams("parallel", "arbitrary", "arbitrary"),
        name="moe",
    )(h, x, comb, wg, wu, wd, g3, g4)


def _ple_kernel(h_ref, p_ref, x_ref, wg_ref, wp_ref, g5_ref, gn_ref, xo_ref, ho_ref):
    gate = _sigmoid(jnp.dot(h_ref[...], wg_ref[...], preferred_element_type=f32))
    emb = jnp.dot(p_ref[...].astype(bf16), wp_ref[...], preferred_element_type=f32)
    x_new = x_ref[...] + _rms(gate * emb, g5_ref[...])
    xo_ref[...] = x_new
    ho_ref[...] = _rms(x_new, gn_ref[...]).astype(ho_ref.dtype)


def ple(h, p, x, wg, wp, g5, gn, tm):
    t, d = x.shape
    dp = p.shape[1]
    row = lambda i: (i, 0)
    const = lambda i: (0, 0)
    return pl.pallas_call(
        _ple_kernel,
        grid=(t // tm,),
        in_specs=[pl.BlockSpec((tm, d), row), pl.BlockSpec((tm, dp), row), pl.BlockSpec((tm, d), row),
                  pl.BlockSpec((d, d), const), pl.BlockSpec((dp, d), const),
                  pl.BlockSpec((1, d), const), pl.BlockSpec((1, d), const)],
        out_specs=[pl.BlockSpec((tm, d), row), pl.BlockSpec((tm, d), row)],
        out_shape=[jax.ShapeDtypeStruct((t, d), f32), jax.ShapeDtypeStruct((t, d), bf16)],
        compiler_params=_cparams("parallel"),
        name="ple",
    )(h, p, x, wg, wp, g5, gn)


def _gate_weights(wa, wx):
    per = MXU_DIM // wa.shape[1]

    def tiles(w):
        groups = w.reshape(w.shape[0] // per, per, w.shape[1], w.shape[2])
        eye = jnp.eye(per, dtype=w.dtype)
        return jnp.einsum('gpcd,pq->gpcqd', groups, eye).reshape(groups.shape[0], MXU_DIM, MXU_DIM)

    return jnp.concatenate([tiles(wa), tiles(wx)], axis=-1).astype(bf16)


def kernel(x, p, positions, norm_g, w_in, conv_w, conv_b, lru_wa, lru_ba, lru_wx, lru_bx, lru_lam,
           diff_lambda, subln_g, w_proj_a, w_proj_b, w_out, ffn_w_gate, ffn_w_up, ffn_w_down, router_w,
           moe_w_gate, moe_w_up, moe_w_down, ple_w_proj, ple_w_gate):
    batch, seq, d = x.shape
    depth = w_in.shape[0]
    t = batch * seq
    tm = min(1024, t)
    tq = min(512, seq)
    tf = 512 if ffn_w_gate.shape[-1] % 512 == 0 else ffn_w_gate.shape[-1]

    xf = x.reshape(t, d)
    pos_f = positions.reshape(t, 1).astype(f32)
    inv_freq = ROPE_THETA ** (-jnp.arange(0, HEAD_DIM, 2, dtype=f32) / HEAD_DIM)
    inv_row = jnp.tile(inv_freq, LANES // (HEAD_DIM // 2)).reshape(1, LANES)
    cos, sin = rope_tables(pos_f, inv_row, tm)
    row = lambda v: v.reshape(1, -1)

    h = norm_cast(xf, row(norm_g[0, 0]), tm)
    for i in range(depth):
        lambda_init = 0.8 - 0.6 * math.exp(-0.3 * i)
        proj = in_proj(h, w_in[i].astype(bf16), cos, sin, tm, d)
        ya = rglru(proj, conv_w[i], row(conv_b[i]), _gate_weights(lru_wa[i], lru_wx[i]),
                   row(lru_ba[i]), row(lru_bx[i]), row(lru_lam[i]), batch, tq)
        yb = diff_attn(proj, diff_lambda[i], row(subln_g[i]), batch, tq, lambda_init)
        xf, h = merge(ya, yb, proj, xf, w_proj_a[i].astype(bf16), w_proj_b[i].astype(bf16),
                      w_out[i].astype(bf16), row(norm_g[i, 1]), row(norm_g[i, 2]), tm)
        if i % 2 == 0:
            xf, h = ffn(h, xf, ffn_w_gate[i // 2].astype(bf16), ffn_w_up[i // 2].astype(bf16),
                        ffn_w_down[i // 2].astype(bf16), row(norm_g[i, 3]), row(norm_g[i, 4]), tm, tf)
        else:
            comb = router(h, router_w[i // 2].T.astype(bf16), tm)
            xf, h = moe_dense(h, xf, comb.T, moe_w_gate[i // 2].astype(bf16), moe_w_up[i // 2].astype(bf16),
                              moe_w_down[i // 2].astype(bf16), row(norm_g[i, 3]), row(norm_g[i, 4]), tm, tf)
        g_next = norm_g[i + 1, 0] if i + 1 < depth else norm_g[i, 0]
        xf, h = ple(h, p[i].reshape(t, -1), xf, ple_w_gate[i].astype(bf16), ple_w_proj[i].astype(bf16),
                    row(norm_g[i, 5]), row(g_next), tm)
    return xf.reshape(batch, seq, d)
```

```python
import functools
import math

import jax
import jax.numpy as jnp
from jax import lax
from jax.experimental import pallas as pl
from jax.experimental.pallas import tpu as pltpu

EPS = 1e-6
CONV_WIDTH = 4
LRU_C = 8.0
N_RNN_BLOCKS = 16
N_HEADS = 8
HEAD_DIM = 64
V_DIM = 2 * HEAD_DIM
ROPE_THETA = 10000.0
TOP_K = 2
LANES = 128
SUBLANES = 8
MXU_DIM = 256
VMEM_LIMIT = 56 * 1024 * 1024
NEG_BIG = -1e30
ATTN_SCALE = HEAD_DIM ** -0.5 * math.log2(math.e)

f32 = jnp.float32
bf16 = jnp.bfloat16


def _cparams(*sem):
    return pltpu.CompilerParams(dimension_semantics=sem, vmem_limit_bytes=VMEM_LIMIT)


def _rms(x, g):
    return x * lax.rsqrt(jnp.mean(x * x, axis=-1, keepdims=True) + EPS) * g


def _sigmoid(x):
    return 1.0 / (1.0 + jnp.exp(-x))


def _rope_tab_kernel(pos_ref, inv_ref, cos_ref, sin_ref):
    ang = pos_ref[...] * inv_ref[...]
    cos_ref[...] = jnp.cos(ang)
    sin_ref[...] = jnp.sin(ang)


def rope_tables(pos_f, inv_row, tm):
    t = pos_f.shape[0]
    return pl.pallas_call(
        _rope_tab_kernel,
        grid=(t // tm,),
        in_specs=[pl.BlockSpec((tm, 1), lambda i: (i, 0)),
                  pl.BlockSpec((1, LANES), lambda i: (0, 0))],
        out_specs=[pl.BlockSpec((tm, LANES), lambda i: (i, 0))] * 2,
        out_shape=[jax.ShapeDtypeStruct((t, LANES), f32)] * 2,
        compiler_params=_cparams("parallel"),
        name="rope_tables",
    )(pos_f, inv_row)


def _norm_kernel(x_ref, g_ref, o_ref):
    o_ref[...] = _rms(x_ref[...], g_ref[...]).astype(o_ref.dtype)


def norm_cast(x, g, tm):
    t, d = x.shape
    return pl.pallas_call(
        _norm_kernel,
        grid=(t // tm,),
        in_specs=[pl.BlockSpec((tm, d), lambda i: (i, 0)),
                  pl.BlockSpec((1, d), lambda i: (0, 0))],
        out_specs=pl.BlockSpec((tm, d), lambda i: (i, 0)),
        out_shape=jax.ShapeDtypeStruct((t, d), bf16),
        compiler_params=_cparams("parallel"),
        name="norm_cast",
    )(x, g)


def _in_proj_kernel(h_ref, w_ref, cos_ref, sin_ref, o_ref, *, q_tile, k_tile):
    j = pl.program_id(1)
    acc = jnp.dot(h_ref[...], w_ref[...], preferred_element_type=f32)
    is_rope = jnp.logical_or(j == q_tile, j == k_tile)

    @pl.when(jnp.logical_not(is_rope))
    def _():
        o_ref[...] = acc.astype(o_ref.dtype)

    @pl.when(is_rope)
    def _():
        scale = jnp.where(j == q_tile, ATTN_SCALE, 1.0).astype(f32)
        cos = cos_ref[...] * scale
        sin = sin_ref[...] * scale
        lane = lax.broadcasted_iota(jnp.int32, cos.shape, 1)
        first_half = (lane % HEAD_DIM) < (HEAD_DIM // 2)
        for g in range(acc.shape[1] // LANES):
            t = acc[:, g * LANES:(g + 1) * LANES]
            up = pltpu.roll(t, HEAD_DIM // 2, axis=1)
            down = pltpu.roll(t, LANES - HEAD_DIM // 2, axis=1)
            rot = jnp.where(first_half, -down, up)
            o_ref[:, g * LANES:(g + 1) * LANES] = (t * cos + rot * sin).astype(o_ref.dtype)


def in_proj(h, w, cos, sin, tm, tn):
    t, d = h.shape
    n = w.shape[1]
    kern = functools.partial(_in_proj_kernel, q_tile=2 * d // tn, k_tile=3 * d // tn)
    return pl.pallas_call(
        kern,
        grid=(t // tm, n // tn),
        in_specs=[pl.BlockSpec((tm, d), lambda i, j: (i, 0)),
                  pl.BlockSpec((d, tn), lambda i, j: (0, j)),
                  pl.BlockSpec((tm, LANES), lambda i, j: (i, 0)),
                  pl.BlockSpec((tm, LANES), lambda i, j: (i, 0))],
        out_specs=pl.BlockSpec((tm, tn), lambda i, j: (i, j)),
        out_shape=jax.ShapeDtypeStruct((t, n), bf16),
        compiler_params=_cparams("parallel", "arbitrary"),
        name="in_proj",
    )(h, w, cos, sin)


def _rglru_kernel(gate_ref, x_ref, cw_ref, cb_ref, wbd_ref, ba_ref, bx_ref, lam_ref, o_ref,
                  xbuf, a_s, b_s, hcar, *, tc):
    c = pl.program_id(1)
    d = x_ref.shape[1]

    @pl.when(c == 0)
    def _():
        xbuf[0:SUBLANES, :] = jnp.zeros((SUBLANES, d), f32)
        hcar[...] = jnp.zeros_like(hcar)

    @pl.when(c > 0)
    def _():
        xbuf[0:SUBLANES, :] = xbuf[tc:tc + SUBLANES, :]

    xbuf[SUBLANES:SUBLANES + tc, :] = x_ref[...].astype(f32)

    u = cb_ref[...] + cw_ref[CONV_WIDTH - 1:CONV_WIDTH, :] * xbuf[SUBLANES:SUBLANES + tc, :]
    for k in range(CONV_WIDTH - 1):
        off = SUBLANES - (CONV_WIDTH - 1) + k
        u = u + cw_ref[k:k + 1, :] * xbuf[off:off + tc, :]

    ub = u.astype(bf16)
    z = -lam_ref[...]
    softplus = jnp.maximum(z, 0.0) + jnp.log(1.0 + jnp.exp(-jnp.abs(z)))
    for g in range(d // MXU_DIM):
        sl = slice(g * MXU_DIM, (g + 1) * MXU_DIM)
        zz = jnp.dot(ub[:, sl], wbd_ref[g], preferred_element_type=f32)
        r = _sigmoid(zz[:, :MXU_DIM] + ba_ref[:, sl])
        i_gate = _sigmoid(zz[:, MXU_DIM:] + bx_ref[:, sl])
        log_a = (-LRU_C) * r * softplus[:, sl]
        a = jnp.exp(log_a)
        beta = jnp.sqrt(1.0 - a * a)
        a_s[:, sl] = a
        b_s[:, sl] = beta * (i_gate * u[:, sl])

    def body(t, h):
        h = a_s[pl.ds(t, 1), :] * h + b_s[pl.ds(t, 1), :]
        b_s[pl.ds(t, 1), :] = h
        return h

    hcar[...] = lax.fori_loop(0, tc, body, hcar[...], unroll=8)

    gt = gate_ref[...].astype(f32)
    gelu = 0.5 * gt * (1.0 + jnp.tanh(math.sqrt(2.0 / math.pi) * (gt + 0.044715 * (gt * gt * gt))))
    o_ref[...] = (b_s[...] * gelu).astype(o_ref.dtype)


def rglru(proj, cw, cb, wbd, ba, bx, lam, batch, tc):
    t = proj.shape[0]
    d = cw.shape[1]
    nc = t // batch // tc
    row = lambda b, c: (b * nc + c, 0)
    const2 = lambda b, c: (0, 0)
    return pl.pallas_call(
        functools.partial(_rglru_kernel, tc=tc),
        grid=(batch, nc),
        in_specs=[pl.BlockSpec((tc, d), lambda b, c: (b * nc + c, 0)),
                  pl.BlockSpec((tc, d), lambda b, c: (b * nc + c, 1)),
                  pl.BlockSpec((CONV_WIDTH, d), const2),
                  pl.BlockSpec((1, d), const2),
                  pl.BlockSpec(wbd.shape, lambda b, c: (0, 0, 0)),
                  pl.BlockSpec((1, d), const2),
                  pl.BlockSpec((1, d), const2),
                  pl.BlockSpec((1, d), const2)],
        out_specs=pl.BlockSpec((tc, d), row),
        out_shape=jax.ShapeDtypeStruct((t, d), bf16),
        scratch_shapes=[pltpu.VMEM((tc + SUBLANES, d), f32),
                        pltpu.VMEM((tc, d), f32),
                        pltpu.VMEM((tc, d), f32),
                        pltpu.VMEM((1, d), f32)],
        compiler_params=_cparams("parallel", "arbitrary"),
        name="rglru",
    )(proj, proj, cw, cb, wbd, ba, bx, lam)


def _diff_attn_kernel(q_ref, k_ref, v_ref, dl_ref, gs_ref, o_ref, vt, qt, m_s, l_s, acc, *, tq, lambda_init):
    qi = pl.program_id(2)
    n_kv = k_ref.shape[0] // tq

    @pl.when(qi == 0)
    def _():
        for c in range(n_kv):
            vt[c] = v_ref[c * tq:(c + 1) * tq, :].astype(f32).T.astype(bf16)

    q_t = q_ref[...].astype(f32).T
    row = lax.broadcasted_iota(jnp.int32, q_t.shape, 0)
    qt[0] = jnp.where(row < HEAD_DIM, q_t, 0.0).astype(bf16)
    qt[1] = jnp.where(row >= HEAD_DIM, q_t, 0.0).astype(bf16)

    m_s[...] = jnp.full_like(m_s, NEG_BIG)
    l_s[...] = jnp.zeros_like(l_s)
    acc[...] = jnp.zeros_like(acc)

    def step(j, masked):
        start = pl.multiple_of(j * tq, tq)
        k = k_ref[pl.ds(start, tq), :]
        v_t = vt[j]
        if masked:
            key = lax.broadcasted_iota(jnp.int32, (tq, tq), 0)
            query = lax.broadcasted_iota(jnp.int32, (tq, tq), 1)
            causal = query >= key
        for mi in range(2):
            s_t = jnp.dot(k, qt[mi], preferred_element_type=f32)
            if masked:
                s_t = jnp.where(causal, s_t, NEG_BIG)
            m_old = m_s[mi]
            m_new = jnp.maximum(m_old, jnp.max(s_t, axis=0, keepdims=True))
            alpha = jnp.exp2(m_old - m_new)
            p_t = jnp.exp2(s_t - m_new)
            l_s[mi] = alpha * l_s[mi] + jnp.sum(p_t, axis=0, keepdims=True)
            acc[mi] = alpha * acc[mi] + jnp.dot(v_t, p_t.astype(bf16), preferred_element_type=f32)
            m_s[mi] = m_new

    def body(j, carry):
        step(j, False)
        return carry

    lax.fori_loop(0, qi, body, 0)
    step(qi, True)

    dl = dl_ref[...]
    lam = (jnp.exp(jnp.sum(dl[0:1, :] * dl[1:2, :], axis=-1, keepdims=True))
           - jnp.exp(jnp.sum(dl[2:3, :] * dl[3:4, :], axis=-1, keepdims=True)) + lambda_init)
    o_t = acc[0] / l_s[0] - lam * (acc[1] / l_s[1])
    o_t = o_t * lax.rsqrt(jnp.mean(o_t * o_t, axis=0, keepdims=True) + EPS)
    o_ref[...] = (o_t.T * gs_ref[...] * (1.0 - lambda_init)).astype(o_ref.dtype)


def diff_attn(proj, dl, gs, batch, tq, lambda_init):
    t = proj.shape[0]
    seq = t // batch
    nq = seq // tq
    d = N_HEADS * V_DIM
    q_col, k_col, v_col = 2 * N_HEADS, 3 * N_HEADS, 4 * N_HEADS
    kern = functools.partial(_diff_attn_kernel, tq=tq, lambda_init=lambda_init)
    return pl.pallas_call(
        kern,
        grid=(batch, N_HEADS, nq),
        in_specs=[pl.BlockSpec((tq, V_DIM), lambda b, h, i: (b * nq + i, q_col + h)),
                  pl.BlockSpec((seq, V_DIM), lambda b, h, i: (b, k_col + h)),
                  pl.BlockSpec((seq, V_DIM), lambda b, h, i: (b, v_col + h)),
                  pl.BlockSpec((4, HEAD_DIM), lambda b, h, i: (0, 0)),
                  pl.BlockSpec((1, V_DIM), lambda b, h, i: (0, 0))],
        out_specs=pl.BlockSpec((tq, V_DIM), lambda b, h, i: (b * nq + i, h)),
        out_shape=jax.ShapeDtypeStruct((t, d), bf16),
        scratch_shapes=[pltpu.VMEM((nq, V_DIM, tq), bf16),
                        pltpu.VMEM((2, V_DIM, tq), bf16),
                        pltpu.VMEM((2, 1, tq), f32),
                        pltpu.VMEM((2, 1, tq), f32),
                        pltpu.VMEM((2, V_DIM, tq), f32)],
        compiler_params=_cparams("parallel", "parallel", "arbitrary"),
        name="diff_attn",
    )(proj, proj, proj, dl, gs)


def _merge_kernel(ya_ref, yb_ref, ga_ref, gb_ref, x_ref, wa_ref, wb_ref, wo_ref, g1_ref, g2_ref,
                  xo_ref, ho_ref):
    pa = jnp.dot(ya_ref[...], wa_ref[...], preferred_element_type=f32)
    pb = jnp.dot(yb_ref[...], wb_ref[...], preferred_element_type=f32)
    merged = _sigmoid(ga_ref[...].astype(f32)) * pa + _sigmoid(gb_ref[...].astype(f32)) * pb
    o = jnp.dot(merged.astype(bf16), wo_ref[...], preferred_element_type=f32)
    x_new = x_ref[...] + _rms(o, g1_ref[...])
    xo_ref[...] = x_new
    ho_ref[...] = _rms(x_new, g2_ref[...]).astype(ho_ref.dtype)


def merge(ya, yb, proj, x, wa, wb, wo, g1, g2, tm, h_dtype):
    t, d = x.shape
    ga_col = proj.shape[1] // d - 2
    row = lambda i: (i, 0)
    const = lambda i: (0, 0)
    return pl.pallas_call(
        _merge_kernel,
        grid=(t // tm,),
        in_specs=[pl.BlockSpec((tm, d), row), pl.BlockSpec((tm, d), row),
                  pl.BlockSpec((tm, d), lambda i: (i, ga_col)),
                  pl.BlockSpec((tm, d), lambda i: (i, ga_col + 1)),
                  pl.BlockSpec((tm, d), row),
                  pl.BlockSpec((d, d), const), pl.BlockSpec((d, d), const), pl.BlockSpec((d, d), const),
                  pl.BlockSpec((1, d), const), pl.BlockSpec((1, d), const)],
        out_specs=[pl.BlockSpec((tm, d), row), pl.BlockSpec((tm, d), row)],
        out_shape=[jax.ShapeDtypeStruct((t, d), f32), jax.ShapeDtypeStruct((t, d), h_dtype)],
        compiler_params=_cparams("parallel"),
        name="merge",
    )(ya, yb, proj, proj, x, wa, wb, wo, g1, g2)


def _ffn_kernel(h_ref, x_ref, wg_ref, wu_ref, wd_ref, g3_ref, g4_ref, xo_ref, ho_ref, acc):
    f = pl.program_id(1)
    h = h_ref[...]
    gate = jnp.dot(h, wg_ref[...], preferred_element_type=f32)
    up = jnp.dot(h, wu_ref[...], preferred_element_type=f32)
    act = (gate * _sigmoid(gate) * up).astype(bf16)
    part = jnp.dot(act, wd_ref[...], preferred_element_type=f32)

    @pl.when(f == 0)
    def _():
        acc[...] = part

    @pl.when(f > 0)
    def _():
        acc[...] += part

    @pl.when(f == pl.num_programs(1) - 1)
    def _():
        x_new = x_ref[...] + _rms(acc[...], g3_ref[...])
        xo_ref[...] = x_new
        ho_ref[...] = _rms(x_new, g4_ref[...]).astype(ho_ref.dtype)


def ffn(h, x, wg, wu, wd, g3, g4, tm, tf):
    t, d = x.shape
    dff = wg.shape[1]
    row = lambda i, f: (i, 0)
    const = lambda i, f: (0, 0)
    return pl.pallas_call(
        _ffn_kernel,
        grid=(t // tm, dff // tf),
        in_specs=[pl.BlockSpec((tm, d), row), pl.BlockSpec((tm, d), row),
                  pl.BlockSpec((d, tf), lambda i, f: (0, f)),
                  pl.BlockSpec((d, tf), lambda i, f: (0, f)),
                  pl.BlockSpec((tf, d), lambda i, f: (f, 0)),
                  pl.BlockSpec((1, d), const), pl.BlockSpec((1, d), const)],
        out_specs=[pl.BlockSpec((tm, d), row), pl.BlockSpec((tm, d), row)],
        out_shape=[jax.ShapeDtypeStruct((t, d), f32), jax.ShapeDtypeStruct((t, d), bf16)],
        scratch_shapes=[pltpu.VMEM((tm, d), f32)],
        compiler_params=_cparams("parallel", "arbitrary"),
        name="ffn",
    )(h, x, wg, wu, wd, g3, g4)


def _router_kernel(h_ref, rw_ref, idx_ref, w_ref, cnt_ref, tri, cnt):
    tm = h_ref.shape[0]

    @pl.when(pl.program_id(0) == 0)
    def _():
        before = lax.broadcasted_iota(jnp.int32, (tm, tm), 0) < lax.broadcasted_iota(jnp.int32, (tm, tm), 1)
        tri[...] = jnp.where(before, 1.0, 0.0).astype(bf16)
        cnt[...] = jnp.zeros_like(cnt)

    logits = lax.dot_general(rw_ref[...], h_ref[...].astype(bf16), (((1,), (1,)), ((), ())),
                             preferred_element_type=f32)
    e_idx = lax.broadcasted_iota(jnp.int32, logits.shape, 0)
    n_e = logits.shape[0]
    v1 = jnp.max(logits, axis=0, keepdims=True)
    i1 = jnp.min(jnp.where(logits == v1, e_idx, n_e), axis=0, keepdims=True)
    rest = jnp.where(e_idx == i1, -jnp.inf, logits)
    v2 = jnp.max(rest, axis=0, keepdims=True)
    i2 = jnp.min(jnp.where(rest == v2, e_idx, n_e), axis=0, keepdims=True)
    ex = jnp.exp(v2 - v1)
    w_ref[0:1, :] = 1.0 / (1.0 + ex)
    w_ref[1:2, :] = ex / (1.0 + ex)

    sel1 = e_idx == i1
    sel2 = e_idx == i2
    sel = jnp.where(jnp.logical_or(sel1, sel2), 1.0, 0.0)
    prefix = jnp.dot(sel.astype(bf16), tri[...], preferred_element_type=f32)
    rank = prefix + cnt[:, 0:1]
    idx_ref[0:1, :] = i1
    idx_ref[1:2, :] = i2
    idx_ref[2:3, :] = jnp.sum(jnp.where(sel1, rank, 0.0), axis=0, keepdims=True).astype(jnp.int32)
    idx_ref[3:4, :] = jnp.sum(jnp.where(sel2, rank, 0.0), axis=0, keepdims=True).astype(jnp.int32)
    cnt[...] = cnt[...] + jnp.sum(sel, axis=1, keepdims=True)
    cnt_ref[...] = cnt[...]


def router(h, rw_t, tm):
    t, d = h.shape
    n_e = rw_t.shape[0]
    return pl.pallas_call(
        _router_kernel,
        grid=(t // tm,),
        in_specs=[pl.BlockSpec((tm, d), lambda i: (i, 0)),
                  pl.BlockSpec((n_e, d), lambda i: (0, 0))],
        out_specs=[pl.BlockSpec((4, tm), lambda i: (0, i)),
                   pl.BlockSpec((2, tm), lambda i: (0, i)),
                   pl.BlockSpec((n_e, LANES), lambda i: (0, 0))],
        out_shape=[jax.ShapeDtypeStruct((4, t), jnp.int32),
                   jax.ShapeDtypeStruct((2, t), f32),
                   jax.ShapeDtypeStruct((n_e, LANES), f32)],
        scratch_shapes=[pltpu.VMEM((tm, tm), bf16), pltpu.VMEM((n_e, LANES), f32)],
        compiler_params=_cparams("arbitrary"),
        name="router",
    )(h, rw_t)


def _dispatch_kernel(slot_ref, h_hbm, xs_in, xs_hbm, sem, *, tc):
    del xs_in
    base = pl.program_id(0) * tc

    def body(t, carry):
        src = h_hbm.at[pl.ds(base + t, 1)]
        for k in range(TOP_K):
            pltpu.make_async_copy(src, xs_hbm.at[pl.ds(slot_ref[k, t], 1)], sem).start()
        return carry

    lax.fori_loop(0, tc, body, 0, unroll=8)
    pltpu.make_async_copy(h_hbm.at[pl.ds(0, TOP_K * tc)], xs_hbm.at[pl.ds(0, TOP_K * tc)], sem).wait()


def dispatch(slots, h, n_slots, tc):
    t, d = h.shape
    return pl.pallas_call(
        functools.partial(_dispatch_kernel, tc=tc),
        grid=(t // tc,),
        in_specs=[pl.BlockSpec((TOP_K, tc), lambda i: (0, i), memory_space=pltpu.SMEM),
                  pl.BlockSpec(memory_space=pl.ANY),
                  pl.BlockSpec(memory_space=pl.ANY)],
        out_specs=pl.BlockSpec(memory_space=pl.ANY),
        out_shape=jax.ShapeDtypeStruct((n_slots, d), h.dtype),
        scratch_shapes=[pltpu.SemaphoreType.DMA(())],
        input_output_aliases={2: 0},
        compiler_params=_cparams("arbitrary"),
        name="dispatch",
    )(slots, h, jnp.zeros((n_slots, d), h.dtype))


def _experts_kernel(te_ref, rt_ref, nu_ref, xs_ref, wg_ref, wu_ref, wd_ref, ys_ref, acc):
    del te_ref, rt_ref
    f = pl.program_id(1)

    @pl.when(pl.program_id(0) < nu_ref[0])
    def _():
        h = xs_ref[...].astype(bf16)
        gate = jnp.dot(h, wg_ref[0], preferred_element_type=f32)
        up = jnp.dot(h, wu_ref[0], preferred_element_type=f32)
        act = (gate * _sigmoid(gate) * up).astype(bf16)
        part = jnp.dot(act, wd_ref[0], preferred_element_type=f32)

        @pl.when(f == 0)
        def _():
            acc[...] = part

        @pl.when(f > 0)
        def _():
            acc[...] += part

        @pl.when(f == pl.num_programs(1) - 1)
        def _():
            ys_ref[...] = acc[...]

    @pl.when(pl.program_id(0) >= nu_ref[0])
    def _():
        ys_ref[...] = jnp.zeros_like(ys_ref)


def experts(tile_expert, row_tile, n_used, xs, wg, wu, wd, te, tf):
    n_slots, d = xs.shape
    dff = wg.shape[2]
    nf = dff // tf
    f_of = lambda i, f, nu: jnp.where(i < nu[0], f, nf - 1)
    return pl.pallas_call(
        _experts_kernel,
        grid_spec=pltpu.PrefetchScalarGridSpec(
            num_scalar_prefetch=3,
            grid=(n_slots // te, nf),
            in_specs=[pl.BlockSpec((te, d), lambda i, f, et, rt, nu: (rt[i], 0)),
                      pl.BlockSpec((1, d, tf), lambda i, f, et, rt, nu: (et[i], 0, f_of(i, f, nu))),
                      pl.BlockSpec((1, d, tf), lambda i, f, et, rt, nu: (et[i], 0, f_of(i, f, nu))),
                      pl.BlockSpec((1, tf, d), lambda i, f, et, rt, nu: (et[i], f_of(i, f, nu), 0))],
            out_specs=pl.BlockSpec((te, d), lambda i, f, et, rt, nu: (i, 0)),
            scratch_shapes=[pltpu.VMEM((te, d), f32)]),
        out_shape=jax.ShapeDtypeStruct((n_slots, d), f32),
        compiler_params=_cparams("arbitrary", "arbitrary"),
        name="experts",
    )(tile_expert, row_tile, n_used, xs, wg, wu, wd)


def _combine_kernel(slot_ref, ys_hbm, w_ref, x_ref, g3_ref, g4_ref, xo_ref, ho_ref, buf, sem):
    tm = x_ref.shape[0]

    def body(t, carry):
        for k in range(TOP_K):
            pltpu.make_async_copy(ys_hbm.at[pl.ds(slot_ref[k, t], 1)], buf.at[k, pl.ds(t, 1)], sem).start()
        return carry

    lax.fori_loop(0, tm, body, 0, unroll=8)
    for k in range(TOP_K):
        pltpu.make_async_copy(ys_hbm.at[pl.ds(0, tm)], buf.at[k], sem).wait()

    w = w_ref[...]
    mixed = w[:, 0:1] * buf[0] + w[:, 1:2] * buf[1]
    x_new = x_ref[...] + _rms(mixed, g3_ref[...])
    xo_ref[...] = x_new
    ho_ref[...] = _rms(x_new, g4_ref[...]).astype(ho_ref.dtype)


def combine(slots, ys, w_cols, x, g3, g4, tm):
    t, d = x.shape
    row = lambda i: (i, 0)
    const = lambda i: (0, 0)
    return pl.pallas_call(
        _combine_kernel,
        grid=(t // tm,),
        in_specs=[pl.BlockSpec((TOP_K, tm), lambda i: (0, i), memory_space=pltpu.SMEM),
                  pl.BlockSpec(memory_space=pl.ANY),
                  pl.BlockSpec((tm, TOP_K), row), pl.BlockSpec((tm, d), row),
                  pl.BlockSpec((1, d), const), pl.BlockSpec((1, d), const)],
        out_specs=[pl.BlockSpec((tm, d), row), pl.BlockSpec((tm, d), row)],
        out_shape=[jax.ShapeDtypeStruct((t, d), f32), jax.ShapeDtypeStruct((t, d), bf16)],
        scratch_shapes=[pltpu.VMEM((TOP_K, tm, d), f32), pltpu.SemaphoreType.DMA(())],
        compiler_params=_cparams("arbitrary"),
        name="combine",
    )(slots, ys, w_cols, x, g3, g4)


def moe_routed(h, x, rw_t, wg, wu, wd, g3, g4, tm, te, tf):
    t, d = h.shape
    n_e = wg.shape[0]
    idx, w_rows, cnt = router(h, rw_t, tm)
    counts = cnt[:, 0].astype(jnp.int32)
    padded = (counts + te - 1) // te * te
    ends = jnp.cumsum(padded)
    offs = ends - padded
    slots = jnp.stack([offs[idx[0]] + idx[2], offs[idx[1]] + idx[3]])
    n_tiles = TOP_K * t // te + n_e
    n_used = ends[-1] // te
    tiles = jnp.minimum(jnp.arange(n_tiles, dtype=jnp.int32), n_used - 1)
    tile_expert = jnp.sum((tiles[:, None] >= (ends // te)[None, :]).astype(jnp.int32), axis=1)
    xs = dispatch(slots, h, n_tiles * te, tm)
    ys = experts(tile_expert, tiles, n_used.reshape(1).astype(jnp.int32), xs, wg, wu, wd, te, tf)
    return combine(slots, ys, w_rows.T, x, g3, g4, min(tm, 512))


def _ple_kernel(h_ref, p_ref, x_ref, wg_ref, wp_ref, g5_ref, gn_ref, xo_ref, ho_ref):
    gate = _sigmoid(jnp.dot(h_ref[...], wg_ref[...], preferred_element_type=f32))
    emb = jnp.dot(p_ref[...].astype(bf16), wp_ref[...], preferred_element_type=f32)
    x_new = x_ref[...] + _rms(gate * emb, g5_ref[...])
    xo_ref[...] = x_new
    ho_ref[...] = _rms(x_new, gn_ref[...]).astype(ho_ref.dtype)


def ple(h, p, x, wg, wp, g5, gn, tm):
    t, d = x.shape
    dp = p.shape[1]
    row = lambda i: (i, 0)
    const = lambda i: (0, 0)
    return pl.pallas_call(
        _ple_kernel,
        grid=(t // tm,),
        in_specs=[pl.BlockSpec((tm, d), row), pl.BlockSpec((tm, dp), row), pl.BlockSpec((tm, d), row),
                  pl.BlockSpec((d, d), const), pl.BlockSpec((dp, d), const),
                  pl.BlockSpec((1, d), const), pl.BlockSpec((1, d), const)],
        out_specs=[pl.BlockSpec((tm, d), row), pl.BlockSpec((tm, d), row)],
        out_shape=[jax.ShapeDtypeStruct((t, d), f32), jax.ShapeDtypeStruct((t, d), bf16)],
        compiler_params=_cparams("parallel"),
        name="ple",
    )(h, p, x, wg, wp, g5, gn)


def _gate_weights(wa, wx):
    per = MXU_DIM // wa.shape[1]

    def tiles(w):
        groups = w.reshape(w.shape[0] // per, per, w.shape[1], w.shape[2])
        eye = jnp.eye(per, dtype=w.dtype)
        return jnp.einsum('gpcd,pq->gpcqd', groups, eye).reshape(groups.shape[0], MXU_DIM, MXU_DIM)

    return jnp.concatenate([tiles(wa), tiles(wx)], axis=-1).astype(bf16)


def kernel(x, p, positions, norm_g, w_in, conv_w, conv_b, lru_wa, lru_ba, lru_wx, lru_bx, lru_lam,
           diff_lambda, subln_g, w_proj_a, w_proj_b, w_out, ffn_w_gate, ffn_w_up, ffn_w_down, router_w,
           moe_w_gate, moe_w_up, moe_w_down, ple_w_proj, ple_w_gate):
    batch, seq, d = x.shape
    depth = w_in.shape[0]
    t = batch * seq
    tm = min(1024, t)
    tq = min(512, seq)
    te = 512
    tf =512 if ffn_w_gate.shape[-1] % 512 == 0 else ffn_w_gate.shape[-1]

    xf = x.reshape(t, d)
    pos_f = positions.reshape(t, 1).astype(f32)
    inv_freq = ROPE_THETA ** (-jnp.arange(0, HEAD_DIM, 2, dtype=f32) / HEAD_DIM)
    inv_row = jnp.tile(inv_freq, LANES // (HEAD_DIM // 2)).reshape(1, LANES)
    cos, sin = rope_tables(pos_f, inv_row, tm)
    row = lambda v: v.reshape(1, -1)

    h = norm_cast(xf, row(norm_g[0, 0]), tm)
    for i in range(depth):
        lambda_init = 0.8 - 0.6 * math.exp(-0.3 * i)
        proj = in_proj(h, w_in[i].astype(bf16), cos, sin, tm, d)
        ya = rglru(proj, conv_w[i], row(conv_b[i]), _gate_weights(lru_wa[i], lru_wx[i]),
                   row(lru_ba[i]), row(lru_bx[i]), row(lru_lam[i]), batch, tq)
        yb = diff_attn(proj, diff_lambda[i], row(subln_g[i]), batch, tq, lambda_init)
        dense = i % 2 == 0
        xf, h = merge(ya, yb, proj, xf, w_proj_a[i].astype(bf16), w_proj_b[i].astype(bf16),
                      w_out[i].astype(bf16), row(norm_g[i, 1]), row(norm_g[i, 2]), tm, bf16 if dense else f32)
        if dense:
            xf, h = ffn(h, xf, ffn_w_gate[i // 2].astype(bf16), ffn_w_up[i // 2].astype(bf16),
                        ffn_w_down[i // 2].astype(bf16), row(norm_g[i, 3]), row(norm_g[i, 4]), tm, tf)
        else:
            xf, h = moe_routed(h, xf, router_w[i // 2].T.astype(bf16), moe_w_gate[i // 2].astype(bf16),
                               moe_w_up[i // 2].astype(bf16), moe_w_down[i // 2].astype(bf16),
                               row(norm_g[i, 3]), row(norm_g[i, 4]), tm, te, tf)
        g_next = norm_g[i + 1, 0] if i + 1 < depth else norm_g[i, 0]
        xf, h = ple(h, p[i].reshape(t, -1), xf, ple_w_gate[i].astype(bf16), ple_w_proj[i].astype(bf16),
                    row(norm_g[i, 5]), row(g_next), tm)
    return xf.reshape(batch, seq, d)
```

```python
import functools
import math

import jax
import jax.numpy as jnp
from jax import lax
from jax.experimental import pallas as pl
from jax.experimental.pallas import tpu as pltpu

EPS = 1e-6
CONV_WIDTH = 4
LRU_C = 8.0
N_RNN_BLOCKS = 16
N_HEADS = 8
HEAD_DIM = 64
V_DIM = 2 * HEAD_DIM
ROPE_THETA = 10000.0
TOP_K = 2
LANES = 128
SUBLANES = 8
MXU_DIM = 256
VMEM_LIMIT = 56 * 1024 * 1024
NEG_BIG = -1e30
ATTN_SCALE = HEAD_DIM ** -0.5 * math.log2(math.e)

f32 = jnp.float32
bf16 = jnp.bfloat16


def _cparams(*sem):
    return pltpu.CompilerParams(dimension_semantics=sem, vmem_limit_bytes=VMEM_LIMIT)


def _rms(x, g):
    return x * lax.rsqrt(jnp.mean(x * x, axis=-1, keepdims=True) + EPS) * g


def _sigmoid(x):
    return 1.0 / (1.0 + jnp.exp(-x))


def _rope_tab_kernel(pos_ref, inv_ref, cos_ref, sin_ref):
    ang = pos_ref[...] * inv_ref[...]
    cos_ref[...] = jnp.cos(ang)
    sin_ref[...] = jnp.sin(ang)


def rope_tables(pos_f, inv_row, tm):
    t = pos_f.shape[0]
    return pl.pallas_call(
        _rope_tab_kernel,
        grid=(t // tm,),
        in_specs=[pl.BlockSpec((tm, 1), lambda i: (i, 0)),
                  pl.BlockSpec((1, LANES), lambda i: (0, 0))],
        out_specs=[pl.BlockSpec((tm, LANES), lambda i: (i, 0))] * 2,
        out_shape=[jax.ShapeDtypeStruct((t, LANES), f32)] * 2,
        compiler_params=_cparams("parallel"),
        name="rope_tables",
    )(pos_f, inv_row)


def _norm_kernel(x_ref, g_ref, o_ref):
    o_ref[...] = _rms(x_ref[...], g_ref[...]).astype(o_ref.dtype)


def norm_cast(x, g, tm):
    t, d = x.shape
    return pl.pallas_call(
        _norm_kernel,
        grid=(t // tm,),
        in_specs=[pl.BlockSpec((tm, d), lambda i: (i, 0)),
                  pl.BlockSpec((1, d), lambda i: (0, 0))],
        out_specs=pl.BlockSpec((tm, d), lambda i: (i, 0)),
        out_shape=jax.ShapeDtypeStruct((t, d), bf16),
        compiler_params=_cparams("parallel"),
        name="norm_cast",
    )(x, g)


def _in_proj_kernel(h_ref, w_ref, cos_ref, sin_ref, o_ref, *, q_tile, k_tile):
    j = pl.program_id(1)
    acc = jnp.dot(h_ref[...], w_ref[...], preferred_element_type=f32)
    is_rope = jnp.logical_or(j == q_tile, j == k_tile)

    @pl.when(jnp.logical_not(is_rope))
    def _():
        o_ref[...] = acc.astype(o_ref.dtype)

    @pl.when(is_rope)
    def _():
        scale = jnp.where(j == q_tile, ATTN_SCALE, 1.0).astype(f32)
        cos = cos_ref[...] * scale
        sin = sin_ref[...] * scale
        lane = lax.broadcasted_iota(jnp.int32, cos.shape, 1)
        first_half = (lane % HEAD_DIM) < (HEAD_DIM // 2)
        for g in range(acc.shape[1] // LANES):
            t = acc[:, g * LANES:(g + 1) * LANES]
            up = pltpu.roll(t, HEAD_DIM // 2, axis=1)
            down = pltpu.roll(t, LANES - HEAD_DIM // 2, axis=1)
            rot = jnp.where(first_half, -down, up)
            o_ref[:, g * LANES:(g + 1) * LANES] = (t * cos + rot * sin).astype(o_ref.dtype)


def in_proj(h, w, cos, sin, tm, tn):
    t, d = h.shape
    n = w.shape[1]
    kern = functools.partial(_in_proj_kernel, q_tile=2 * d // tn, k_tile=3 * d // tn)
    return pl.pallas_call(
        kern,
        grid=(t // tm, n // tn),
        in_specs=[pl.BlockSpec((tm, d), lambda i, j: (i, 0)),
                  pl.BlockSpec((d, tn), lambda i, j: (0, j)),
                  pl.BlockSpec((tm, LANES), lambda i, j: (i, 0)),
                  pl.BlockSpec((tm, LANES), lambda i, j: (i, 0))],
        out_specs=pl.BlockSpec((tm, tn), lambda i, j: (i, j)),
        out_shape=jax.ShapeDtypeStruct((t, n), bf16),
        compiler_params=_cparams("parallel", "arbitrary"),
        name="in_proj",
    )(h, w, cos, sin)


def _rglru_kernel(gate_ref, x_ref, cw_ref, cb_ref, wbd_ref, ba_ref, bx_ref, lam_ref, o_ref,
                  xbuf, a_s, b_s, hcar, *, tc):
    c = pl.program_id(1)
    d = x_ref.shape[1]

    @pl.when(c == 0)
    def _():
        xbuf[0:SUBLANES, :] = jnp.zeros((SUBLANES, d), f32)
        hcar[...] = jnp.zeros_like(hcar)

    @pl.when(c > 0)
    def _():
        xbuf[0:SUBLANES, :] = xbuf[tc:tc + SUBLANES, :]

    xbuf[SUBLANES:SUBLANES + tc, :] = x_ref[...].astype(f32)

    u = cb_ref[...] + cw_ref[CONV_WIDTH - 1:CONV_WIDTH, :] * xbuf[SUBLANES:SUBLANES + tc, :]
    for k in range(CONV_WIDTH - 1):
        off = SUBLANES - (CONV_WIDTH - 1) + k
        u = u + cw_ref[k:k + 1, :] * xbuf[off:off + tc, :]

    ub = u.astype(bf16)
    z = -lam_ref[...]
    softplus = jnp.maximum(z, 0.0) + jnp.log(1.0 + jnp.exp(-jnp.abs(z)))
    for g in range(d // MXU_DIM):
        sl = slice(g * MXU_DIM, (g + 1) * MXU_DIM)
        zz = jnp.dot(ub[:, sl], wbd_ref[g], preferred_element_type=f32)
        r = _sigmoid(zz[:, :MXU_DIM] + ba_ref[:, sl])
        i_gate = _sigmoid(zz[:, MXU_DIM:] + bx_ref[:, sl])
        log_a = (-LRU_C) * r * softplus[:, sl]
        a = jnp.exp(log_a)
        beta = jnp.sqrt(1.0 - a * a)
        a_s[:, sl] = a
        b_s[:, sl] = beta * (i_gate * u[:, sl])

    def body(t, h):
        h = a_s[pl.ds(t, 1), :] * h + b_s[pl.ds(t, 1), :]
        b_s[pl.ds(t, 1), :] = h
        return h

    hcar[...] = lax.fori_loop(0, tc, body, hcar[...], unroll=8)

    gt = gate_ref[...].astype(f32)
    gelu = 0.5 * gt * (1.0 + jnp.tanh(math.sqrt(2.0 / math.pi) * (gt + 0.044715 * (gt * gt * gt))))
    o_ref[...] = (b_s[...] * gelu).astype(o_ref.dtype)


def rglru(proj, cw, cb, wbd, ba, bx, lam, batch, tc):
    t = proj.shape[0]
    d = cw.shape[1]
    nc = t // batch // tc
    row = lambda b, c: (b * nc + c, 0)
    const2 = lambda b, c: (0, 0)
    return pl.pallas_call(
        functools.partial(_rglru_kernel, tc=tc),
        grid=(batch, nc),
        in_specs=[pl.BlockSpec((tc, d), lambda b, c: (b * nc + c, 0)),
                  pl.BlockSpec((tc, d), lambda b, c: (b * nc + c, 1)),
                  pl.BlockSpec((CONV_WIDTH, d), const2),
                  pl.BlockSpec((1, d), const2),
                  pl.BlockSpec(wbd.shape, lambda b, c: (0, 0, 0)),
                  pl.BlockSpec((1, d), const2),
                  pl.BlockSpec((1, d), const2),
                  pl.BlockSpec((1, d), const2)],
        out_specs=pl.BlockSpec((tc, d), row),
        out_shape=jax.ShapeDtypeStruct((t, d), bf16),
        scratch_shapes=[pltpu.VMEM((tc + SUBLANES, d), f32),
                        pltpu.VMEM((tc, d), f32),
                        pltpu.VMEM((tc, d), f32),
                        pltpu.VMEM((1, d), f32)],
        compiler_params=_cparams("parallel", "arbitrary"),
        name="rglru",
    )(proj, proj, cw, cb, wbd, ba, bx, lam)


def _diff_attn_kernel(q_ref, k_ref, v_ref, dl_ref, gs_ref, o_ref, vt, qt, s_a, s_b, m_s, l_s, acc,
                      *, tq, lambda_init):
    qi = pl.program_id(2)
    n_kv = k_ref.shape[0] // tq

    @pl.when(qi == 0)
    def _():
        for c in range(n_kv):
            vt[c] = v_ref[c * tq:(c + 1) * tq, :].astype(f32).T.astype(bf16)

    q_t = q_ref[...].astype(f32).T
    row = lax.broadcasted_iota(jnp.int32, q_t.shape, 0)
    qt[:, 0:tq] = jnp.where(row < HEAD_DIM, q_t, 0.0).astype(bf16)
    qt[:, tq:2 * tq] = jnp.where(row >= HEAD_DIM, q_t, 0.0).astype(bf16)

    m_s[...] = jnp.full_like(m_s, NEG_BIG)
    l_s[...] = jnp.zeros_like(l_s)
    acc[...] = jnp.zeros_like(acc)

    def scores(j, s_ref):
        start = pl.multiple_of(j * tq, tq)
        s_ref[...] = jnp.dot(k_ref[pl.ds(start, tq), :], qt[...], preferred_element_type=f32)

    def consume(j, s_ref, masked):
        s_t = s_ref[...]
        if masked:
            key = lax.broadcasted_iota(jnp.int32, s_t.shape, 0)
            query = lax.broadcasted_iota(jnp.int32, s_t.shape, 1) % tq
            s_t = jnp.where(query >= key, s_t, NEG_BIG)
        m_old = m_s[...]
        m_new = jnp.maximum(m_old, jnp.max(s_t, axis=0, keepdims=True))
        alpha = jnp.exp2(m_old - m_new)
        p_t = jnp.exp2(s_t - m_new)
        l_s[...] = alpha * l_s[...] + jnp.sum(p_t, axis=0, keepdims=True)
        acc[...] = alpha * acc[...] + jnp.dot(vt[j], p_t.astype(bf16), preferred_element_type=f32)
        m_s[...] = m_new

    scores(0, s_a)

    def pair(i, carry):
        j = 2 * i
        scores(j + 1, s_b)
        consume(j, s_a, False)
        scores(j + 2, s_a)
        consume(j + 1, s_b, False)
        return carry

    lax.fori_loop(0, qi // 2, pair, 0)

    @pl.when(qi % 2 == 1)
    def _():
        scores(qi, s_b)
        consume(qi - 1, s_a, False)
        consume(qi, s_b, True)

    @pl.when(qi % 2 == 0)
    def _():
        consume(qi, s_a, True)

    dl = dl_ref[...]
    lam = (jnp.exp(jnp.sum(dl[0:1, :] * dl[1:2, :], axis=-1, keepdims=True))
           - jnp.exp(jnp.sum(dl[2:3, :] * dl[3:4, :], axis=-1, keepdims=True)) + lambda_init)
    o_all = acc[...] / l_s[...]
    o_t = o_all[:, 0:tq] - lam * o_all[:, tq:2 * tq]
    o_t = o_t * lax.rsqrt(jnp.mean(o_t * o_t, axis=0, keepdims=True) + EPS)
    o_ref[...] = (o_t.T * gs_ref[...] * (1.0 - lambda_init)).astype(o_ref.dtype)


def diff_attn(proj, dl, gs, batch, tq, lambda_init):
    t = proj.shape[0]
    seq = t // batch
    nq = seq // tq
    d = N_HEADS * V_DIM
    q_col, k_col, v_col = 2 * N_HEADS, 3 * N_HEADS, 4 * N_HEADS
    kern = functools.partial(_diff_attn_kernel, tq=tq, lambda_init=lambda_init)
    return pl.pallas_call(
        kern,
        grid=(batch, N_HEADS, nq),
        in_specs=[pl.BlockSpec((tq, V_DIM), lambda b, h, i: (b * nq + i, q_col + h)),
                  pl.BlockSpec((seq, V_DIM), lambda b, h, i: (b, k_col + h)),
                  pl.BlockSpec((seq, V_DIM), lambda b, h, i: (b, v_col + h)),
                  pl.BlockSpec((4, HEAD_DIM), lambda b, h, i: (0, 0)),
                  pl.BlockSpec((1, V_DIM), lambda b, h, i: (0, 0))],
        out_specs=pl.BlockSpec((tq, V_DIM), lambda b, h, i: (b * nq + i, h)),
        out_shape=jax.ShapeDtypeStruct((t, d), bf16),
        scratch_shapes=[pltpu.VMEM((nq, V_DIM, tq), bf16),
                        pltpu.VMEM((V_DIM, 2 * tq), bf16),
                        pltpu.VMEM((tq, 2 * tq), f32),
                        pltpu.VMEM((tq, 2 * tq), f32),
                        pltpu.VMEM((1, 2 * tq), f32),
                        pltpu.VMEM((1, 2 * tq), f32),
                        pltpu.VMEM((V_DIM, 2 * tq), f32)],
        compiler_params=_cparams("parallel", "parallel", "arbitrary"),
        name="diff_attn",
    )(proj, proj, proj, dl, gs)


def _merge_kernel(ya_ref, yb_ref, ga_ref, gb_ref, x_ref, wa_ref, wb_ref, wo_ref, g1_ref, g2_ref,
                  xo_ref, ho_ref):
    pa = jnp.dot(ya_ref[...], wa_ref[...], preferred_element_type=f32)
    pb = jnp.dot(yb_ref[...], wb_ref[...], preferred_element_type=f32)
    merged = _sigmoid(ga_ref[...].astype(f32)) * pa + _sigmoid(gb_ref[...].astype(f32)) * pb
    o = jnp.dot(merged.astype(bf16), wo_ref[...], preferred_element_type=f32)
    x_new = x_ref[...] + _rms(o, g1_ref[...])
    xo_ref[...] = x_new
    ho_ref[...] = _rms(x_new, g2_ref[...]).astype(ho_ref.dtype)


def merge(ya, yb, proj, x, wa, wb, wo, g1, g2, tm, h_dtype):
    t, d = x.shape
    ga_col = proj.shape[1] // d - 2
    row = lambda i: (i, 0)
    const = lambda i: (0, 0)
    return pl.pallas_call(
        _merge_kernel,
        grid=(t // tm,),
        in_specs=[pl.BlockSpec((tm, d), row), pl.BlockSpec((tm, d), row),
                  pl.BlockSpec((tm, d), lambda i: (i, ga_col)),
                  pl.BlockSpec((tm, d), lambda i: (i, ga_col + 1)),
                  pl.BlockSpec((tm, d), row),
                  pl.BlockSpec((d, d), const), pl.BlockSpec((d, d), const), pl.BlockSpec((d, d), const),
                  pl.BlockSpec((1, d), const), pl.BlockSpec((1, d), const)],
        out_specs=[pl.BlockSpec((tm, d), row), pl.BlockSpec((tm, d), row)],
        out_shape=[jax.ShapeDtypeStruct((t, d), f32), jax.ShapeDtypeStruct((t, d), h_dtype)],
        compiler_params=_cparams("parallel"),
        name="merge",
    )(ya, yb, proj, proj, x, wa, wb, wo, g1, g2)


def _ffn_kernel(h_ref, x_ref, wg_ref, wu_ref, wd_ref, g3_ref, g4_ref, xo_ref, ho_ref, acc):
    f = pl.program_id(1)
    h = h_ref[...]
    gate = jnp.dot(h, wg_ref[...], preferred_element_type=f32)
    up = jnp.dot(h, wu_ref[...], preferred_element_type=f32)
    act = (gate * _sigmoid(gate) * up).astype(bf16)
    part = jnp.dot(act, wd_ref[...], preferred_element_type=f32)

    @pl.when(f == 0)
    def _():
        acc[...] = part

    @pl.when(f > 0)
    def _():
        acc[...] += part

    @pl.when(f == pl.num_programs(1) - 1)
    def _():
        x_new = x_ref[...] + _rms(acc[...], g3_ref[...])
        xo_ref[...] = x_new
        ho_ref[...] = _rms(x_new, g4_ref[...]).astype(ho_ref.dtype)


def ffn(h, x, wg, wu, wd, g3, g4, tm, tf):
    t, d = x.shape
    dff = wg.shape[1]
    row = lambda i, f: (i, 0)
    const = lambda i, f: (0, 0)
    return pl.pallas_call(
        _ffn_kernel,
        grid=(t // tm, dff // tf),
        in_specs=[pl.BlockSpec((tm, d), row), pl.BlockSpec((tm, d), row),
                  pl.BlockSpec((d, tf), lambda i, f: (0, f)),
                  pl.BlockSpec((d, tf), lambda i, f: (0, f)),
                  pl.BlockSpec((tf, d), lambda i, f: (f, 0)),
                  pl.BlockSpec((1, d), const), pl.BlockSpec((1, d), const)],
        out_specs=[pl.BlockSpec((tm, d), row), pl.BlockSpec((tm, d), row)],
        out_shape=[jax.ShapeDtypeStruct((t, d), f32), jax.ShapeDtypeStruct((t, d), bf16)],
        scratch_shapes=[pltpu.VMEM((tm, d), f32)],
        compiler_params=_cparams("parallel", "arbitrary"),
        name="ffn",
    )(h, x, wg, wu, wd, g3, g4)


def _router_kernel(h_ref, rw_ref, idx_ref, w_ref, cnt_ref, tri, cnt):
    tm = h_ref.shape[0]

    @pl.when(pl.program_id(0) == 0)
    def _():
        before = lax.broadcasted_iota(jnp.int32, (tm, tm), 0) < lax.broadcasted_iota(jnp.int32, (tm, tm), 1)
        tri[...] = jnp.where(before, 1.0, 0.0).astype(bf16)
        cnt[...] = jnp.zeros_like(cnt)

    logits = lax.dot_general(rw_ref[...], h_ref[...].astype(bf16), (((1,), (1,)), ((), ())),
                             preferred_element_type=f32)
    e_idx = lax.broadcasted_iota(jnp.int32, logits.shape, 0)
    n_e = logits.shape[0]
    v1 = jnp.max(logits, axis=0, keepdims=True)
    i1 = jnp.min(jnp.where(logits == v1, e_idx, n_e), axis=0, keepdims=True)
    rest = jnp.where(e_idx == i1, -jnp.inf, logits)
    v2 = jnp.max(rest, axis=0, keepdims=True)
    i2 = jnp.min(jnp.where(rest == v2, e_idx, n_e), axis=0, keepdims=True)
    ex = jnp.exp(v2 - v1)
    w_ref[0:1, :] = 1.0 / (1.0 + ex)
    w_ref[1:2, :] = ex / (1.0 + ex)

    sel1 = e_idx == i1
    sel2 = e_idx == i2
    sel = jnp.where(jnp.logical_or(sel1, sel2), 1.0, 0.0)
    prefix = jnp.dot(sel.astype(bf16), tri[...], preferred_element_type=f32)
    rank = prefix + cnt[:, 0:1]
    idx_ref[0:1, :] = i1
    idx_ref[1:2, :] = i2
    idx_ref[2:3, :] = jnp.sum(jnp.where(sel1, rank, 0.0), axis=0, keepdims=True).astype(jnp.int32)
    idx_ref[3:4, :] = jnp.sum(jnp.where(sel2, rank, 0.0), axis=0, keepdims=True).astype(jnp.int32)
    cnt[...] = cnt[...] + jnp.sum(sel, axis=1, keepdims=True)
    cnt_ref[...] = cnt[...]


def router(h, rw_t, tm):
    t, d = h.shape
    n_e = rw_t.shape[0]
    return pl.pallas_call(
        _router_kernel,
        grid=(t // tm,),
        in_specs=[pl.BlockSpec((tm, d), lambda i: (i, 0)),
                  pl.BlockSpec((n_e, d), lambda i: (0, 0))],
        out_specs=[pl.BlockSpec((4, tm), lambda i: (0, i)),
                   pl.BlockSpec((2, tm), lambda i: (0, i)),
                   pl.BlockSpec((n_e, LANES), lambda i: (0, 0))],
        out_shape=[jax.ShapeDtypeStruct((4, t), jnp.int32),
                   jax.ShapeDtypeStruct((2, t), f32),
                   jax.ShapeDtypeStruct((n_e, LANES), f32)],
        scratch_shapes=[pltpu.VMEM((tm, tm), bf16), pltpu.VMEM((n_e, LANES), f32)],
        compiler_params=_cparams("arbitrary"),
        name="router",
    )(h, rw_t)


def _dispatch_kernel(slot_ref, h_ref, xs_in, xs_hbm, sem):
    del xs_in
    tc = h_ref.shape[0]

    def body(t, carry):
        for k in range(TOP_K):
            pltpu.make_async_copy(h_ref.at[pl.ds(t, 1)], xs_hbm.at[pl.ds(slot_ref[k, t], 1)], sem).start()
        return carry

    lax.fori_loop(0, tc, body, 0, unroll=8)
    for k in range(TOP_K):
        pltpu.make_async_copy(h_ref, xs_hbm.at[pl.ds(0, tc)], sem).wait()


def dispatch(slots, h, n_slots, tc):
    t, d = h.shape
    return pl.pallas_call(
        _dispatch_kernel,
        grid=(t // tc,),
        in_specs=[pl.BlockSpec((TOP_K, tc), lambda i: (0, i), memory_space=pltpu.SMEM),
                  pl.BlockSpec((tc, d), lambda i: (i, 0)),
                  pl.BlockSpec(memory_space=pl.ANY)],
        out_specs=pl.BlockSpec(memory_space=pl.ANY),
        out_shape=jax.ShapeDtypeStruct((n_slots, d), h.dtype),
        scratch_shapes=[pltpu.SemaphoreType.DMA(())],
        input_output_aliases={2: 0},
        compiler_params=_cparams("arbitrary"),
        name="dispatch",
    )(slots, h, jnp.zeros((n_slots, d), h.dtype))


def _experts_kernel(te_ref, rt_ref, nu_ref, xs_ref, wg_ref, wu_ref, wd_ref, ys_ref, acc):
    del te_ref, rt_ref
    f = pl.program_id(1)

    @pl.when(pl.program_id(0) < nu_ref[0])
    def _():
        h = xs_ref[...].astype(bf16)
        gate = jnp.dot(h, wg_ref[0], preferred_element_type=f32)
        up = jnp.dot(h, wu_ref[0], preferred_element_type=f32)
        act = (gate * _sigmoid(gate) * up).astype(bf16)
        part = jnp.dot(act, wd_ref[0], preferred_element_type=f32)

        @pl.when(f == 0)
        def _():
            acc[...] = part

        @pl.when(f > 0)
        def _():
            acc[...] += part

        @pl.when(f == pl.num_programs(1) - 1)
        def _():
            ys_ref[...] = acc[...]

    @pl.when(pl.program_id(0) >= nu_ref[0])
    def _():
        ys_ref[...] = jnp.zeros_like(ys_ref)


def experts(tile_expert, row_tile, n_used, xs, wg, wu, wd, te, tf):
    n_slots, d = xs.shape
    dff = wg.shape[2]
    nf = dff // tf
    f_of = lambda i, f, nu: jnp.where(i < nu[0], f, nf - 1)
    return pl.pallas_call(
        _experts_kernel,
        grid_spec=pltpu.PrefetchScalarGridSpec(
            num_scalar_prefetch=3,
            grid=(n_slots // te, nf),
            in_specs=[pl.BlockSpec((te, d), lambda i, f, et, rt, nu: (rt[i], 0)),
                      pl.BlockSpec((1, d, tf), lambda i, f, et, rt, nu: (et[i], 0, f_of(i, f, nu))),
                      pl.BlockSpec((1, d, tf), lambda i, f, et, rt, nu: (et[i], 0, f_of(i, f, nu))),
                      pl.BlockSpec((1, tf, d), lambda i, f, et, rt, nu: (et[i], f_of(i, f, nu), 0))],
            out_specs=pl.BlockSpec((te, d), lambda i, f, et, rt, nu: (i, 0)),
            scratch_shapes=[pltpu.VMEM((te, d), f32)]),
        out_shape=jax.ShapeDtypeStruct((n_slots, d), f32),
        compiler_params=_cparams("arbitrary", "arbitrary"),
        name="experts",
    )(tile_expert, row_tile, n_used, xs, wg, wu, wd)


def _combine_kernel(slot_ref, ys_hbm, w_ref, x_ref, g3_ref, g4_ref, xo_ref, ho_ref, buf, sem):
    tm = x_ref.shape[0]

    def body(t, carry):
        for k in range(TOP_K):
            pltpu.make_async_copy(ys_hbm.at[pl.ds(slot_ref[k, t], 1)], buf.at[k, pl.ds(t, 1)], sem).start()
        return carry

    lax.fori_loop(0, tm, body, 0, unroll=8)
    for k in range(TOP_K):
        pltpu.make_async_copy(ys_hbm.at[pl.ds(0, tm)], buf.at[k], sem).wait()

    w = w_ref[...]
    mixed = w[:, 0:1] * buf[0] + w[:, 1:2] * buf[1]
    x_new = x_ref[...] + _rms(mixed, g3_ref[...])
    xo_ref[...] = x_new
    ho_ref[...] = _rms(x_new, g4_ref[...]).astype(ho_ref.dtype)


def combine(slots, ys, w_cols, x, g3, g4, tm):
    t, d = x.shape
    row = lambda i: (i, 0)
    const = lambda i: (0, 0)
    return pl.pallas_call(
        _combine_kernel,
        grid=(t // tm,),
        in_specs=[pl.BlockSpec((TOP_K, tm), lambda i: (0, i), memory_space=pltpu.SMEM),
                  pl.BlockSpec(memory_space=pl.ANY),
                  pl.BlockSpec((tm, TOP_K), row), pl.BlockSpec((tm, d), row),
                  pl.BlockSpec((1, d), const), pl.BlockSpec((1, d), const)],
        out_specs=[pl.BlockSpec((tm, d), row), pl.BlockSpec((tm, d), row)],
        out_shape=[jax.ShapeDtypeStruct((t, d), f32), jax.ShapeDtypeStruct((t, d), bf16)],
        scratch_shapes=[pltpu.VMEM((TOP_K, tm, d), f32), pltpu.SemaphoreType.DMA(())],
        compiler_params=_cparams("arbitrary"),
        name="combine",
    )(slots, ys, w_cols, x, g3, g4)


def moe_routed(h, x, rw_t, wg, wu, wd, g3, g4, tm, te, tf):
    t, d = h.shape
    n_e = wg.shape[0]
    idx, w_rows, cnt = router(h, rw_t, tm)
    counts = cnt[:, 0].astype(jnp.int32)
    padded = (counts + te - 1) // te * te
    ends = jnp.cumsum(padded)
    offs = ends - padded
    slots = jnp.stack([offs[idx[0]] + idx[2], offs[idx[1]] + idx[3]])
    n_tiles = TOP_K * t // te + n_e
    n_used = ends[-1] // te
    tiles = jnp.minimum(jnp.arange(n_tiles, dtype=jnp.int32), n_used - 1)
    tile_expert = jnp.sum((tiles[:, None] >= (ends // te)[None, :]).astype(jnp.int32), axis=1)
    xs = dispatch(slots, h, n_tiles * te, tm)
    ys = experts(tile_expert, tiles, n_used.reshape(1).astype(jnp.int32), xs, wg, wu, wd, te, tf)
    return combine(slots, ys, w_rows.T, x, g3, g4, min(tm, 512))


def _ple_kernel(h_ref, p_ref, x_ref, wg_ref, wp_ref, g5_ref, gn_ref, xo_ref, ho_ref):
    gate = _sigmoid(jnp.dot(h_ref[...], wg_ref[...], preferred_element_type=f32))
    emb = jnp.dot(p_ref[...].astype(bf16), wp_ref[...], preferred_element_type=f32)
    x_new = x_ref[...] + _rms(gate * emb, g5_ref[...])
    xo_ref[...] = x_new
    ho_ref[...] = _rms(x_new, gn_ref[...]).astype(ho_ref.dtype)


def ple(h, p, x, wg, wp, g5, gn, tm):
    t, d = x.shape
    dp = p.shape[1]
    row = lambda i: (i, 0)
    const = lambda i: (0, 0)
    return pl.pallas_call(
        _ple_kernel,
        grid=(t // tm,),
        in_specs=[pl.BlockSpec((tm, d), row), pl.BlockSpec((tm, dp), row), pl.BlockSpec((tm, d), row),
                  pl.BlockSpec((d, d), const), pl.BlockSpec((dp, d), const),
                  pl.BlockSpec((1, d), const), pl.BlockSpec((1, d), const)],
        out_specs=[pl.BlockSpec((tm, d), row), pl.BlockSpec((tm, d), row)],
        out_shape=[jax.ShapeDtypeStruct((t, d), f32), jax.ShapeDtypeStruct((t, d), bf16)],
        compiler_params=_cparams("parallel"),
        name="ple",
    )(h, p, x, wg, wp, g5, gn)


def _gate_weights(wa, wx):
    per = MXU_DIM // wa.shape[1]

    def tiles(w):
        groups = w.reshape(w.shape[0] // per, per, w.shape[1], w.shape[2])
        eye = jnp.eye(per, dtype=w.dtype)
        return jnp.einsum('gpcd,pq->gpcqd', groups, eye).reshape(groups.shape[0], MXU_DIM, MXU_DIM)

    return jnp.concatenate([tiles(wa), tiles(wx)], axis=-1).astype(bf16)


def kernel(x, p, positions, norm_g, w_in, conv_w, conv_b, lru_wa, lru_ba, lru_wx, lru_bx, lru_lam,
           diff_lambda, subln_g, w_proj_a, w_proj_b, w_out, ffn_w_gate, ffn_w_up, ffn_w_down, router_w,
           moe_w_gate, moe_w_up, moe_w_down, ple_w_proj, ple_w_gate):
    batch, seq, d = x.shape
    depth = w_in.shape[0]
    t = batch * seq
    tm = min(1024, t)
    tq = min(512, seq)
    te = 512
    tf =512 if ffn_w_gate.shape[-1] % 512 == 0 else ffn_w_gate.shape[-1]

    xf = x.reshape(t, d)
    pos_f = positions.reshape(t, 1).astype(f32)
    inv_freq = ROPE_THETA ** (-jnp.arange(0, HEAD_DIM, 2, dtype=f32) / HEAD_DIM)
    inv_row = jnp.tile(inv_freq, LANES // (HEAD_DIM // 2)).reshape(1, LANES)
    cos, sin = rope_tables(pos_f, inv_row, tm)
    row = lambda v: v.reshape(1, -1)

    h = norm_cast(xf, row(norm_g[0, 0]), tm)
    for i in range(depth):
        lambda_init = 0.8 - 0.6 * math.exp(-0.3 * i)
        proj = in_proj(h, w_in[i].astype(bf16), cos, sin, tm, d)
        ya = rglru(proj, conv_w[i], row(conv_b[i]), _gate_weights(lru_wa[i], lru_wx[i]),
                   row(lru_ba[i]), row(lru_bx[i]), row(lru_lam[i]), batch, tq)
        yb = diff_attn(proj, diff_lambda[i], row(subln_g[i]), batch, tq, lambda_init)
        dense = i % 2 == 0
        xf, h = merge(ya, yb, proj, xf, w_proj_a[i].astype(bf16), w_proj_b[i].astype(bf16),
                      w_out[i].astype(bf16), row(norm_g[i, 1]), row(norm_g[i, 2]), tm, bf16 if dense else f32)
        if dense:
            xf, h = ffn(h, xf, ffn_w_gate[i // 2].astype(bf16), ffn_w_up[i // 2].astype(bf16),
                        ffn_w_down[i // 2].astype(bf16), row(norm_g[i, 3]), row(norm_g[i, 4]), tm, tf)
        else:
            xf, h = moe_routed(h, xf, router_w[i // 2].T.astype(bf16), moe_w_gate[i // 2].astype(bf16),
                               moe_w_up[i // 2].astype(bf16), moe_w_down[i // 2].astype(bf16),
                               row(norm_g[i, 3]), row(norm_g[i, 4]), tm, te, tf)
        g_next = norm_g[i + 1, 0] if i + 1 < depth else norm_g[i, 0]
        xf, h = ple(h, p[i].reshape(t, -1), xf, ple_w_gate[i].astype(bf16), ple_w_proj[i].astype(bf16),
                    row(norm_g[i, 5]), row(g_next), tm)
    return xf.reshape(batch, seq, d)
```

```python
import functools
import math

import jax
import jax.numpy as jnp
from jax import lax
from jax.experimental import pallas as pl
from jax.experimental.pallas import tpu as pltpu

EPS = 1e-6
CONV_WIDTH = 4
LRU_C = 8.0
N_RNN_BLOCKS = 16
N_HEADS = 8
HEAD_DIM = 64
V_DIM = 2 * HEAD_DIM
ROPE_THETA = 10000.0
TOP_K = 2
LANES = 128
SUBLANES = 8
MXU_DIM = 256
VMEM_LIMIT = 56 * 1024 * 1024
NEG_BIG = -1e30
ATTN_SCALE = HEAD_DIM ** -0.5 * math.log2(math.e)

f32 = jnp.float32
bf16 = jnp.bfloat16


def _cparams(*sem):
    return pltpu.CompilerParams(dimension_semantics=sem, vmem_limit_bytes=VMEM_LIMIT)


def _rms(x, g):
    return x * lax.rsqrt(jnp.mean(x * x, axis=-1, keepdims=True) + EPS) * g


def _sigmoid(x):
    return 1.0 / (1.0 + jnp.exp(-x))


def _rope_tab_kernel(pos_ref, inv_ref, cos_ref, sin_ref):
    ang = pos_ref[...] * inv_ref[...]
    cos_ref[...] = jnp.cos(ang)
    sin_ref[...] = jnp.sin(ang)


def rope_tables(pos_f, inv_row, tm):
    t = pos_f.shape[0]
    return pl.pallas_call(
        _rope_tab_kernel,
        grid=(t // tm,),
        in_specs=[pl.BlockSpec((tm, 1), lambda i: (i, 0)),
                  pl.BlockSpec((1, LANES), lambda i: (0, 0))],
        out_specs=[pl.BlockSpec((tm, LANES), lambda i: (i, 0))] * 2,
        out_shape=[jax.ShapeDtypeStruct((t, LANES), f32)] * 2,
        compiler_params=_cparams("parallel"),
        name="rope_tables",
    )(pos_f, inv_row)


def _norm_kernel(x_ref, g_ref, o_ref):
    o_ref[...] = _rms(x_ref[...], g_ref[...]).astype(o_ref.dtype)


def norm_cast(x, g, tm):
    t, d = x.shape
    return pl.pallas_call(
        _norm_kernel,
        grid=(t // tm,),
        in_specs=[pl.BlockSpec((tm, d), lambda i: (i, 0)),
                  pl.BlockSpec((1, d), lambda i: (0, 0))],
        out_specs=pl.BlockSpec((tm, d), lambda i: (i, 0)),
        out_shape=jax.ShapeDtypeStruct((t, d), bf16),
        compiler_params=_cparams("parallel"),
        name="norm_cast",
    )(x, g)


def _in_proj_kernel(h_ref, w_ref, cos_ref, sin_ref, o_ref, *, tn, q_tile, k_tile):
    h = h_ref[...]
    for j in range(w_ref.shape[1] // tn):
        cols = slice(j * tn, (j + 1) * tn)
        acc = jnp.dot(h, w_ref[:, cols], preferred_element_type=f32)
        if j not in (q_tile, k_tile):
            o_ref[:, cols] = acc.astype(o_ref.dtype)
            continue
        scale = ATTN_SCALE if j == q_tile else 1.0
        cos = cos_ref[...] * scale
        sin = sin_ref[...] * scale
        lane = lax.broadcasted_iota(jnp.int32, cos.shape, 1)
        first_half = (lane % HEAD_DIM) < (HEAD_DIM // 2)
        for g in range(tn // LANES):
            t = acc[:, g * LANES:(g + 1) * LANES]
            up = pltpu.roll(t, HEAD_DIM // 2, axis=1)
            down = pltpu.roll(t, LANES - HEAD_DIM // 2, axis=1)
            rot = jnp.where(first_half, -down, up)
            lanes = slice(j * tn + g * LANES, j * tn + (g + 1) * LANES)
            o_ref[:, lanes] = (t * cos + rot * sin).astype(o_ref.dtype)


def in_proj(h, w, cos, sin, tm, tn):
    t, d = h.shape
    n = w.shape[1]
    kern = functools.partial(_in_proj_kernel, tn=tn, q_tile=2 * d // tn, k_tile=3 * d // tn)
    return pl.pallas_call(
        kern,
        grid=(t // tm,),
        in_specs=[pl.BlockSpec((tm, d), lambda i: (i, 0)),
                  pl.BlockSpec((d, n), lambda i: (0, 0), pipeline_mode=pl.Buffered(1)),
                  pl.BlockSpec((tm, LANES), lambda i: (i, 0)),
                  pl.BlockSpec((tm, LANES), lambda i: (i, 0))],
        out_specs=pl.BlockSpec((tm, n), lambda i: (i, 0)),
        out_shape=jax.ShapeDtypeStruct((t, n), bf16),
        compiler_params=_cparams("parallel"),
        name="in_proj",
    )(h, w, cos, sin)


def _rglru_kernel(gate_ref, x_ref, cw_ref, cb_ref, wbd_ref, ba_ref, bx_ref, lam_ref, o_ref,
                  xbuf, a_s, b_s, hcar, *, tc):
    c = pl.program_id(1)
    d = x_ref.shape[1]

    @pl.when(c == 0)
    def _():
        xbuf[0:SUBLANES, :] = jnp.zeros((SUBLANES, d), f32)
        hcar[...] = jnp.zeros_like(hcar)

    @pl.when(c > 0)
    def _():
        xbuf[0:SUBLANES, :] = xbuf[tc:tc + SUBLANES, :]

    xbuf[SUBLANES:SUBLANES + tc, :] = x_ref[...].astype(f32)

    u = cb_ref[...] + cw_ref[CONV_WIDTH - 1:CONV_WIDTH, :] * xbuf[SUBLANES:SUBLANES + tc, :]
    for k in range(CONV_WIDTH - 1):
        off = SUBLANES - (CONV_WIDTH - 1) + k
        u = u + cw_ref[k:k + 1, :] * xbuf[off:off + tc, :]

    ub = u.astype(bf16)
    z = -lam_ref[...]
    softplus = jnp.maximum(z, 0.0) + jnp.log(1.0 + jnp.exp(-jnp.abs(z)))
    for g in range(d // MXU_DIM):
        sl = slice(g * MXU_DIM, (g + 1) * MXU_DIM)
        zz = jnp.dot(ub[:, sl], wbd_ref[g], preferred_element_type=f32)
        r = _sigmoid(zz[:, :MXU_DIM] + ba_ref[:, sl])
        i_gate = _sigmoid(zz[:, MXU_DIM:] + bx_ref[:, sl])
        log_a = (-LRU_C) * r * softplus[:, sl]
        a = jnp.exp(log_a)
        beta = jnp.sqrt(1.0 - a * a)
        a_s[:, sl] = a
        b_s[:, sl] = beta * (i_gate * u[:, sl])

    def body(t, h):
        h = a_s[pl.ds(t, 1), :] * h + b_s[pl.ds(t, 1), :]
        b_s[pl.ds(t, 1), :] = h
        return h

    hcar[...] = lax.fori_loop(0, tc, body, hcar[...], unroll=8)

    gt = gate_ref[...].astype(f32)
    gelu = 0.5 * gt * (1.0 + jnp.tanh(math.sqrt(2.0 / math.pi) * (gt + 0.044715 * (gt * gt * gt))))
    o_ref[...] = (b_s[...] * gelu).astype(o_ref.dtype)


def rglru(proj, cw, cb, wbd, ba, bx, lam, batch, tc):
    t = proj.shape[0]
    d = cw.shape[1]
    nc = t // batch // tc
    row = lambda b, c: (b * nc + c, 0)
    const2 = lambda b, c: (0, 0)
    return pl.pallas_call(
        functools.partial(_rglru_kernel, tc=tc),
        grid=(batch, nc),
        in_specs=[pl.BlockSpec((tc, d), lambda b, c: (b * nc + c, 0)),
                  pl.BlockSpec((tc, d), lambda b, c: (b * nc + c, 1)),
                  pl.BlockSpec((CONV_WIDTH, d), const2),
                  pl.BlockSpec((1, d), const2),
                  pl.BlockSpec(wbd.shape, lambda b, c: (0, 0, 0)),
                  pl.BlockSpec((1, d), const2),
                  pl.BlockSpec((1, d), const2),
                  pl.BlockSpec((1, d), const2)],
        out_specs=pl.BlockSpec((tc, d), row),
        out_shape=jax.ShapeDtypeStruct((t, d), bf16),
        scratch_shapes=[pltpu.VMEM((tc + SUBLANES, d), f32),
                        pltpu.VMEM((tc, d), f32),
                        pltpu.VMEM((tc, d), f32),
                        pltpu.VMEM((1, d), f32)],
        compiler_params=_cparams("parallel", "arbitrary"),
        name="rglru",
    )(proj, proj, cw, cb, wbd, ba, bx, lam)


def _diff_attn_kernel(q_ref, k_ref, v_ref, dl_ref, gs_ref, o_ref, vt, qt, s_a, s_b, m_s, l_s, acc,
                      *, tq, lambda_init):
    qi = pl.program_id(2)
    n_kv = k_ref.shape[0] // tq

    @pl.when(qi == 0)
    def _():
        for c in range(n_kv):
            vt[c] = v_ref[c * tq:(c + 1) * tq, :].astype(f32).T.astype(bf16)

    q_t = q_ref[...].astype(f32).T
    row = lax.broadcasted_iota(jnp.int32, q_t.shape, 0)
    qt[:, 0:tq] = jnp.where(row < HEAD_DIM, q_t, 0.0).astype(bf16)
    qt[:, tq:2 * tq] = jnp.where(row >= HEAD_DIM, q_t, 0.0).astype(bf16)

    m_s[...] = jnp.full_like(m_s, NEG_BIG)
    l_s[...] = jnp.zeros_like(l_s)
    acc[...] = jnp.zeros_like(acc)

    def scores(j, s_ref):
        start = pl.multiple_of(j * tq, tq)
        s_ref[...] = jnp.dot(k_ref[pl.ds(start, tq), :], qt[...], preferred_element_type=f32)

    def consume(j, s_ref, masked):
        s_t = s_ref[...]
        if masked:
            key = lax.broadcasted_iota(jnp.int32, s_t.shape, 0)
            query = lax.broadcasted_iota(jnp.int32, s_t.shape, 1) % tq
            s_t = jnp.where(query >= key, s_t, NEG_BIG)
        m_old = m_s[...]
        m_new = jnp.maximum(m_old, jnp.max(s_t, axis=0, keepdims=True))
        alpha = jnp.exp2(m_old - m_new)
        p_t = jnp.exp2(s_t - m_new)
        l_s[...] = alpha * l_s[...] + jnp.sum(p_t, axis=0, keepdims=True)
        acc[...] = alpha * acc[...] + jnp.dot(vt[j], p_t.astype(bf16), preferred_element_type=f32)
        m_s[...] = m_new

    scores(0, s_a)

    def pair(i, carry):
        j = 2 * i
        scores(j + 1, s_b)
        consume(j, s_a, False)
        scores(j + 2, s_a)
        consume(j + 1, s_b, False)
        return carry

    lax.fori_loop(0, qi // 2, pair, 0)

    @pl.when(qi % 2 == 1)
    def _():
        scores(qi, s_b)
        consume(qi - 1, s_a, False)
        consume(qi, s_b, True)

    @pl.when(qi % 2 == 0)
    def _():
        consume(qi, s_a, True)

    dl = dl_ref[...]
    lam = (jnp.exp(jnp.sum(dl[0:1, :] * dl[1:2, :], axis=-1, keepdims=True))
           - jnp.exp(jnp.sum(dl[2:3, :] * dl[3:4, :], axis=-1, keepdims=True)) + lambda_init)
    o_all = acc[...] * (1.0 / l_s[...])
    o_t = o_all[:, 0:tq] - lam * o_all[:, tq:2 * tq]
    o_t = o_t * lax.rsqrt(jnp.mean(o_t * o_t, axis=0, keepdims=True) + EPS)
    o_ref[...] = (o_t.T * gs_ref[...] * (1.0 - lambda_init)).astype(o_ref.dtype)


def diff_attn(proj, dl, gs, batch, tq, lambda_init):
    t = proj.shape[0]
    seq = t // batch
    nq = seq // tq
    d = N_HEADS * V_DIM
    q_col, k_col, v_col = 2 * N_HEADS, 3 * N_HEADS, 4 * N_HEADS
    kern = functools.partial(_diff_attn_kernel, tq=tq, lambda_init=lambda_init)
    return pl.pallas_call(
        kern,
        grid=(batch, N_HEADS, nq),
        in_specs=[pl.BlockSpec((tq, V_DIM), lambda b, h, i: (b * nq + i, q_col + h)),
                  pl.BlockSpec((seq, V_DIM), lambda b, h, i: (b, k_col + h)),
                  pl.BlockSpec((seq, V_DIM), lambda b, h, i: (b, v_col + h)),
                  pl.BlockSpec((4, HEAD_DIM), lambda b, h, i: (0, 0)),
                  pl.BlockSpec((1, V_DIM), lambda b, h, i: (0, 0))],
        out_specs=pl.BlockSpec((tq, V_DIM), lambda b, h, i: (b * nq + i, h)),
        out_shape=jax.ShapeDtypeStruct((t, d), bf16),
        scratch_shapes=[pltpu.VMEM((nq, V_DIM, tq), bf16),
                        pltpu.VMEM((V_DIM, 2 * tq), bf16),
                        pltpu.VMEM((tq, 2 * tq), f32),
                        pltpu.VMEM((tq, 2 * tq), f32),
                        pltpu.VMEM((1, 2 * tq), f32),
                        pltpu.VMEM((1, 2 * tq), f32),
                        pltpu.VMEM((V_DIM, 2 * tq), f32)],
        compiler_params=_cparams("parallel", "parallel", "arbitrary"),
        name="diff_attn",
    )(proj, proj, proj, dl, gs)


def _merge_kernel(ya_ref, yb_ref, ga_ref, gb_ref, x_ref, wa_ref, wb_ref, wo_ref, g1_ref, g2_ref,
                  xo_ref, ho_ref):
    pa = jnp.dot(ya_ref[...], wa_ref[...], preferred_element_type=f32)
    pb = jnp.dot(yb_ref[...], wb_ref[...], preferred_element_type=f32)
    merged = _sigmoid(ga_ref[...].astype(f32)) * pa + _sigmoid(gb_ref[...].astype(f32)) * pb
    o = jnp.dot(merged.astype(bf16), wo_ref[...], preferred_element_type=f32)
    x_new = x_ref[...] + _rms(o, g1_ref[...])
    xo_ref[...] = x_new
    ho_ref[...] = _rms(x_new, g2_ref[...]).astype(ho_ref.dtype)


def merge(ya, yb, proj, x, wa, wb, wo, g1, g2, tm, h_dtype):
    t, d = x.shape
    ga_col = proj.shape[1] // d - 2
    row = lambda i: (i, 0)
    const = lambda i: (0, 0)
    return pl.pallas_call(
        _merge_kernel,
        grid=(t // tm,),
        in_specs=[pl.BlockSpec((tm, d), row), pl.BlockSpec((tm, d), row),
                  pl.BlockSpec((tm, d), lambda i: (i, ga_col)),
                  pl.BlockSpec((tm, d), lambda i: (i, ga_col + 1)),
                  pl.BlockSpec((tm, d), row),
                  pl.BlockSpec((d, d), const), pl.BlockSpec((d, d), const), pl.BlockSpec((d, d), const),
                  pl.BlockSpec((1, d), const), pl.BlockSpec((1, d), const)],
        out_specs=[pl.BlockSpec((tm, d), row), pl.BlockSpec((tm, d), row)],
        out_shape=[jax.ShapeDtypeStruct((t, d), f32), jax.ShapeDtypeStruct((t, d), h_dtype)],
        compiler_params=_cparams("parallel"),
        name="merge",
    )(ya, yb, proj, proj, x, wa, wb, wo, g1, g2)


def _swiglu(h, wg_ref, wu_ref, wd_ref, act, tf):
    for c in range(act.shape[1] // tf):
        sl = slice(c * tf, (c + 1) * tf)
        gate = jnp.dot(h, wg_ref[:, sl], preferred_element_type=f32)
        up = jnp.dot(h, wu_ref[:, sl], preferred_element_type=f32)
        act[:, sl] = (gate * _sigmoid(gate) * up).astype(bf16)
    return jnp.dot(act[...], wd_ref[...], preferred_element_type=f32)


def _ffn_kernel(h_ref, x_ref, wg_ref, wu_ref, wd_ref, g3_ref, g4_ref, xo_ref, ho_ref, act, *, tf):
    out = _swiglu(h_ref[...], wg_ref, wu_ref, wd_ref, act, tf)
    x_new = x_ref[...] + _rms(out, g3_ref[...])
    xo_ref[...] = x_new
    ho_ref[...] = _rms(x_new, g4_ref[...]).astype(ho_ref.dtype)


def ffn(h, x, wg, wu, wd, g3, g4, tm, tf):
    t, d = x.shape
    dff = wg.shape[1]
    row = lambda i: (i, 0)
    const = lambda i: (0, 0)
    once = pl.Buffered(1)
    return pl.pallas_call(
        functools.partial(_ffn_kernel, tf=tf),
        grid=(t // tm,),
        in_specs=[pl.BlockSpec((tm, d), row), pl.BlockSpec((tm, d), row),
                  pl.BlockSpec((d, dff), const, pipeline_mode=once),
                  pl.BlockSpec((d, dff), const, pipeline_mode=once),
                  pl.BlockSpec((dff, d), const, pipeline_mode=once),
                  pl.BlockSpec((1, d), const), pl.BlockSpec((1, d), const)],
        out_specs=[pl.BlockSpec((tm, d), row), pl.BlockSpec((tm, d), row)],
        out_shape=[jax.ShapeDtypeStruct((t, d), f32), jax.ShapeDtypeStruct((t, d), bf16)],
        scratch_shapes=[pltpu.VMEM((tm, dff), bf16)],
        compiler_params=_cparams("parallel"),
        name="ffn",
    )(h, x, wg, wu, wd, g3, g4)


def _router_kernel(h_ref, rw_ref, idx_ref, w_ref, cnt_ref, tri, cnt):
    tm = h_ref.shape[0]

    @pl.when(pl.program_id(0) == 0)
    def _():
        before = lax.broadcasted_iota(jnp.int32, (tm, tm), 0) < lax.broadcasted_iota(jnp.int32, (tm, tm), 1)
        tri[...] = jnp.where(before, 1.0, 0.0).astype(bf16)
        cnt[...] = jnp.zeros_like(cnt)

    logits = lax.dot_general(rw_ref[...], h_ref[...].astype(bf16), (((1,), (1,)), ((), ())),
                             preferred_element_type=f32)
    e_idx = lax.broadcasted_iota(jnp.int32, logits.shape, 0)
    n_e = logits.shape[0]
    v1 = jnp.max(logits, axis=0, keepdims=True)
    i1 = jnp.min(jnp.where(logits == v1, e_idx, n_e), axis=0, keepdims=True)
    rest = jnp.where(e_idx == i1, -jnp.inf, logits)
    v2 = jnp.max(rest, axis=0, keepdims=True)
    i2 = jnp.min(jnp.where(rest == v2, e_idx, n_e), axis=0, keepdims=True)
    ex = jnp.exp(v2 - v1)
    w_ref[0:1, :] = 1.0 / (1.0 + ex)
    w_ref[1:2, :] = ex / (1.0 + ex)

    sel1 = e_idx == i1
    sel2 = e_idx == i2
    sel = jnp.where(jnp.logical_or(sel1, sel2), 1.0, 0.0)
    prefix = jnp.dot(sel.astype(bf16), tri[...], preferred_element_type=f32)
    rank = prefix + cnt[:, 0:1]
    idx_ref[0:1, :] = i1
    idx_ref[1:2, :] = i2
    idx_ref[2:3, :] = jnp.sum(jnp.where(sel1, rank, 0.0), axis=0, keepdims=True).astype(jnp.int32)
    idx_ref[3:4, :] = jnp.sum(jnp.where(sel2, rank, 0.0), axis=0, keepdims=True).astype(jnp.int32)
    cnt[...] = cnt[...] + jnp.sum(sel, axis=1, keepdims=True)
    cnt_ref[...] = cnt[...]


def router(h, rw_t, tm):
    t, d = h.shape
    n_e = rw_t.shape[0]
    return pl.pallas_call(
        _router_kernel,
        grid=(t // tm,),
        in_specs=[pl.BlockSpec((tm, d), lambda i: (i, 0)),
                  pl.BlockSpec((n_e, d), lambda i: (0, 0))],
        out_specs=[pl.BlockSpec((4, tm), lambda i: (0, i)),
                   pl.BlockSpec((2, tm), lambda i: (0, i)),
                   pl.BlockSpec((n_e, LANES), lambda i: (0, 0))],
        out_shape=[jax.ShapeDtypeStruct((4, t), jnp.int32),
                   jax.ShapeDtypeStruct((2, t), f32),
                   jax.ShapeDtypeStruct((n_e, LANES), f32)],
        scratch_shapes=[pltpu.VMEM((tm, tm), bf16), pltpu.VMEM((n_e, LANES), f32)],
        compiler_params=_cparams("arbitrary"),
        name="router",
    )(h, rw_t)


def _dispatch_kernel(slot_ref, h_ref, xs_in, xs_hbm, sem):
    del xs_in
    tc = h_ref.shape[0]

    def body(t, carry):
        for k in range(TOP_K):
            pltpu.make_async_copy(h_ref.at[pl.ds(t, 1)], xs_hbm.at[pl.ds(slot_ref[k, t], 1)], sem).start()
        return carry

    lax.fori_loop(0, tc, body, 0, unroll=8)
    for k in range(TOP_K):
        pltpu.make_async_copy(h_ref, xs_hbm.at[pl.ds(0, tc)], sem).wait()


def dispatch(slots, h, n_slots, tc):
    t, d = h.shape
    return pl.pallas_call(
        _dispatch_kernel,
        grid=(t // tc,),
        in_specs=[pl.BlockSpec((TOP_K, tc), lambda i: (0, i), memory_space=pltpu.SMEM),
                  pl.BlockSpec((tc, d), lambda i: (i, 0)),
                  pl.BlockSpec(memory_space=pl.ANY)],
        out_specs=pl.BlockSpec(memory_space=pl.ANY),
        out_shape=jax.ShapeDtypeStruct((n_slots, d), h.dtype),
        scratch_shapes=[pltpu.SemaphoreType.DMA(())],
        input_output_aliases={2: 0},
        compiler_params=_cparams("arbitrary"),
        name="dispatch",
    )(slots, h, jnp.zeros((n_slots, d), h.dtype))


def _experts_kernel(te_ref, rt_ref, nu_ref, xs_ref, wg_ref, wu_ref, wd_ref, ys_ref, act, *, tf):
    del te_ref, rt_ref

    @pl.when(pl.program_id(0) < nu_ref[0])
    def _():
        ys_ref[...] = _swiglu(xs_ref[...].astype(bf16), wg_ref.at[0], wu_ref.at[0], wd_ref.at[0], act, tf)

    @pl.when(pl.program_id(0) >= nu_ref[0])
    def _():
        ys_ref[...] = jnp.zeros_like(ys_ref)


def experts(tile_expert, row_tile, n_used, xs, wg, wu, wd, te, tf):
    n_slots, d = xs.shape
    dff = wg.shape[2]
    once = pl.Buffered(1)
    return pl.pallas_call(
        functools.partial(_experts_kernel, tf=tf),
        grid_spec=pltpu.PrefetchScalarGridSpec(
            num_scalar_prefetch=3,
            grid=(n_slots // te,),
            in_specs=[pl.BlockSpec((te, d), lambda i, et, rt, nu: (rt[i], 0)),
                      pl.BlockSpec((1, d, dff), lambda i, et, rt, nu: (et[i], 0, 0), pipeline_mode=once),
                      pl.BlockSpec((1, d, dff), lambda i, et, rt, nu: (et[i], 0, 0), pipeline_mode=once),
                      pl.BlockSpec((1, dff, d), lambda i, et, rt, nu: (et[i], 0, 0), pipeline_mode=once)],
            out_specs=pl.BlockSpec((te, d), lambda i, et, rt, nu: (i, 0)),
            scratch_shapes=[pltpu.VMEM((te, dff), bf16)]),
        out_shape=jax.ShapeDtypeStruct((n_slots, d), f32),
        compiler_params=_cparams("arbitrary"),
        name="experts",
    )(tile_expert, row_tile, n_used, xs, wg, wu, wd)


def _combine_kernel(slot_ref, ys_hbm, w_ref, x_ref, g3_ref, g4_ref, xo_ref, ho_ref, buf, sem):
    tm = x_ref.shape[0]

    def body(t, carry):
        for k in range(TOP_K):
            pltpu.make_async_copy(ys_hbm.at[pl.ds(slot_ref[k, t], 1)], buf.at[k, pl.ds(t, 1)], sem).start()
        return carry

    lax.fori_loop(0, tm, body, 0, unroll=8)
    for k in range(TOP_K):
        pltpu.make_async_copy(ys_hbm.at[pl.ds(0, tm)], buf.at[k], sem).wait()

    w = w_ref[...]
    mixed = w[:, 0:1] * buf[0] + w[:, 1:2] * buf[1]
    x_new = x_ref[...] + _rms(mixed, g3_ref[...])
    xo_ref[...] = x_new
    ho_ref[...] = _rms(x_new, g4_ref[...]).astype(ho_ref.dtype)


def combine(slots, ys, w_cols, x, g3, g4, tm):
    t, d = x.shape
    row = lambda i: (i, 0)
    const = lambda i: (0, 0)
    return pl.pallas_call(
        _combine_kernel,
        grid=(t // tm,),
        in_specs=[pl.BlockSpec((TOP_K, tm), lambda i: (0, i), memory_space=pltpu.SMEM),
                  pl.BlockSpec(memory_space=pl.ANY),
                  pl.BlockSpec((tm, TOP_K), row), pl.BlockSpec((tm, d), row),
                  pl.BlockSpec((1, d), const), pl.BlockSpec((1, d), const)],
        out_specs=[pl.BlockSpec((tm, d), row), pl.BlockSpec((tm, d), row)],
        out_shape=[jax.ShapeDtypeStruct((t, d), f32), jax.ShapeDtypeStruct((t, d), bf16)],
        scratch_shapes=[pltpu.VMEM((TOP_K, tm, d), f32), pltpu.SemaphoreType.DMA(())],
        compiler_params=_cparams("arbitrary"),
        name="combine",
    )(slots, ys, w_cols, x, g3, g4)


def moe_routed(h, x, rw_t, wg, wu, wd, g3, g4, tm, te, tf):
    t, d = h.shape
    n_e = wg.shape[0]
    idx, w_rows, cnt = router(h, rw_t, tm)
    counts = cnt[:, 0].astype(jnp.int32)
    padded = (counts + te - 1) // te * te
    ends = jnp.cumsum(padded)
    offs = ends - padded
    slots = jnp.stack([offs[idx[0]] + idx[2], offs[idx[1]] + idx[3]])
    n_tiles = TOP_K * t // te + n_e
    n_used = ends[-1] // te
    tiles = jnp.minimum(jnp.arange(n_tiles, dtype=jnp.int32), n_used - 1)
    tile_expert = jnp.sum((tiles[:, None] >= (ends // te)[None, :]).astype(jnp.int32), axis=1)
    xs = dispatch(slots, h, n_tiles * te, tm)
    ys = experts(tile_expert, tiles, n_used.reshape(1).astype(jnp.int32), xs, wg, wu, wd, te, tf)
    return combine(slots, ys, w_rows.T, x, g3, g4, min(tm, 512))


def _ple_kernel(h_ref, p_ref, x_ref, wg_ref, wp_ref, g5_ref, gn_ref, xo_ref, ho_ref):
    gate = _sigmoid(jnp.dot(h_ref[...], wg_ref[...], preferred_element_type=f32))
    emb = jnp.dot(p_ref[...].astype(bf16), wp_ref[...], preferred_element_type=f32)
    x_new = x_ref[...] + _rms(gate * emb, g5_ref[...])
    xo_ref[...] = x_new
    ho_ref[...] = _rms(x_new, gn_ref[...]).astype(ho_ref.dtype)


def ple(h, p, x, wg, wp, g5, gn, tm):
    t, d = x.shape
    dp = p.shape[1]
    row = lambda i: (i, 0)
    const = lambda i: (0, 0)
    return pl.pallas_call(
        _ple_kernel,
        grid=(t // tm,),
        in_specs=[pl.BlockSpec((tm, d), row), pl.BlockSpec((tm, dp), row), pl.BlockSpec((tm, d), row),
                  pl.BlockSpec((d, d), const), pl.BlockSpec((dp, d), const),
                  pl.BlockSpec((1, d), const), pl.BlockSpec((1, d), const)],
        out_specs=[pl.BlockSpec((tm, d), row), pl.BlockSpec((tm, d), row)],
        out_shape=[jax.ShapeDtypeStruct((t, d), f32), jax.ShapeDtypeStruct((t, d), bf16)],
        compiler_params=_cparams("parallel"),
        name="ple",
    )(h, p, x, wg, wp, g5, gn)


def _gate_weights(wa, wx):
    per = MXU_DIM // wa.shape[1]

    def tiles(w):
        groups = w.reshape(w.shape[0] // per, per, w.shape[1], w.shape[2])
        eye = jnp.eye(per, dtype=w.dtype)
        return jnp.einsum('gpcd,pq->gpcqd', groups, eye).reshape(groups.shape[0], MXU_DIM, MXU_DIM)

    return jnp.concatenate([tiles(wa), tiles(wx)], axis=-1).astype(bf16)


def kernel(x, p, positions, norm_g, w_in, conv_w, conv_b, lru_wa, lru_ba, lru_wx, lru_bx, lru_lam,
           diff_lambda, subln_g, w_proj_a, w_proj_b, w_out, ffn_w_gate, ffn_w_up, ffn_w_down, router_w,
           moe_w_gate, moe_w_up, moe_w_down, ple_w_proj, ple_w_gate):
    batch, seq, d = x.shape
    depth = w_in.shape[0]
    t = batch * seq
    tm = min(1024, t)
    tq = min(512, seq)
    te = 512
    tf =512 if ffn_w_gate.shape[-1] % 512 == 0 else ffn_w_gate.shape[-1]

    xf = x.reshape(t, d)
    pos_f = positions.reshape(t, 1).astype(f32)
    inv_freq = ROPE_THETA ** (-jnp.arange(0, HEAD_DIM, 2, dtype=f32) / HEAD_DIM)
    inv_row = jnp.tile(inv_freq, LANES // (HEAD_DIM // 2)).reshape(1, LANES)
    cos, sin = rope_tables(pos_f, inv_row, tm)
    row = lambda v: v.reshape(1, -1)

    h = norm_cast(xf, row(norm_g[0, 0]), tm)
    for i in range(depth):
        lambda_init = 0.8 - 0.6 * math.exp(-0.3 * i)
        proj = in_proj(h, w_in[i].astype(bf16), cos, sin, min(tm, 512), d)
        ya = rglru(proj, conv_w[i], row(conv_b[i]), _gate_weights(lru_wa[i], lru_wx[i]),
                   row(lru_ba[i]), row(lru_bx[i]), row(lru_lam[i]), batch, tq)
        yb = diff_attn(proj, diff_lambda[i], row(subln_g[i]), batch, tq, lambda_init)
        dense = i % 2 == 0
        xf, h = merge(ya, yb, proj, xf, w_proj_a[i].astype(bf16), w_proj_b[i].astype(bf16),
                      w_out[i].astype(bf16), row(norm_g[i, 1]), row(norm_g[i, 2]), tm, bf16 if dense else f32)
        if dense:
            xf, h = ffn(h, xf, ffn_w_gate[i // 2].astype(bf16), ffn_w_up[i // 2].astype(bf16),
                        ffn_w_down[i // 2].astype(bf16), row(norm_g[i, 3]), row(norm_g[i, 4]), min(tm, 512), tf)
        else:
            xf, h = moe_routed(h, xf, router_w[i // 2].T.astype(bf16), moe_w_gate[i // 2].astype(bf16),
                               moe_w_up[i // 2].astype(bf16), moe_w_down[i // 2].astype(bf16),
                               row(norm_g[i, 3]), row(norm_g[i, 4]), tm, te, tf)
        g_next = norm_g[i + 1, 0] if i + 1 < depth else norm_g[i, 0]
        xf, h = ple(h, p[i].reshape(t, -1), xf, ple_w_gate[i].astype(bf16), ple_w_proj[i].astype(bf16),
                    row(norm_g[i, 5]), row(g_next), tm)
    return xf.reshape(batch, seq, d)
```

```python
import functools
import math

import jax
import jax.numpy as jnp
from jax import lax
from jax.experimental import pallas as pl
from jax.experimental.pallas import tpu as pltpu

EPS = 1e-6
CONV_WIDTH = 4
LRU_C = 8.0
N_RNN_BLOCKS = 16
N_HEADS = 8
HEAD_DIM = 64
V_DIM = 2 * HEAD_DIM
ROPE_THETA = 10000.0
TOP_K = 2
LANES = 128
SUBLANES = 8
MXU_DIM = 256
VMEM_LIMIT = 56 * 1024 * 1024
NEG_BIG = -1e30
ATTN_SCALE = HEAD_DIM ** -0.5 * math.log2(math.e)

f32 = jnp.float32
bf16 = jnp.bfloat16


def _cparams(*sem):
    return pltpu.CompilerParams(dimension_semantics=sem, vmem_limit_bytes=VMEM_LIMIT)


def _rms(x, g):
    return x * lax.rsqrt(jnp.mean(x * x, axis=-1, keepdims=True) + EPS) * g


def _sigmoid(x):
    return 1.0 / (1.0 + jnp.exp(-x))


def _sigmoid_tanh(x):
    return 0.5 * (1.0 + jnp.tanh(0.5 * x))


def _rope_tab_kernel(pos_ref, inv_ref, cos_ref, sin_ref):
    ang = pos_ref[...] * inv_ref[...]
    cos_ref[...] = jnp.cos(ang)
    sin_ref[...] = jnp.sin(ang)


def rope_tables(pos_f, inv_row, tm):
    t = pos_f.shape[0]
    return pl.pallas_call(
        _rope_tab_kernel,
        grid=(t // tm,),
        in_specs=[pl.BlockSpec((tm, 1), lambda i: (i, 0)),
                  pl.BlockSpec((1, LANES), lambda i: (0, 0))],
        out_specs=[pl.BlockSpec((tm, LANES), lambda i: (i, 0))] * 2,
        out_shape=[jax.ShapeDtypeStruct((t, LANES), f32)] * 2,
        compiler_params=_cparams("parallel"),
        name="rope_tables",
    )(pos_f, inv_row)


def _norm_kernel(x_ref, g_ref, o_ref):
    o_ref[...] = _rms(x_ref[...], g_ref[...]).astype(o_ref.dtype)


def norm_cast(x, g, tm):
    t, d = x.shape
    return pl.pallas_call(
        _norm_kernel,
        grid=(t // tm,),
        in_specs=[pl.BlockSpec((tm, d), lambda i: (i, 0)),
                  pl.BlockSpec((1, d), lambda i: (0, 0))],
        out_specs=pl.BlockSpec((tm, d), lambda i: (i, 0)),
        out_shape=jax.ShapeDtypeStruct((t, d), bf16),
        compiler_params=_cparams("parallel"),
        name="norm_cast",
    )(x, g)


def _in_proj_kernel(h_ref, w_ref, cos_ref, sin_ref, o_ref, *, tn, q_tile, k_tile):
    h = h_ref[...]
    for j in range(w_ref.shape[1] // tn):
        cols = slice(j * tn, (j + 1) * tn)
        acc = jnp.dot(h, w_ref[:, cols], preferred_element_type=f32)
        if j not in (q_tile, k_tile):
            o_ref[:, cols] = acc.astype(o_ref.dtype)
            continue
        scale = ATTN_SCALE if j == q_tile else 1.0
        cos = cos_ref[...] * scale
        sin = sin_ref[...] * scale
        lane = lax.broadcasted_iota(jnp.int32, cos.shape, 1)
        first_half = (lane % HEAD_DIM) < (HEAD_DIM // 2)
        for g in range(tn // LANES):
            t = acc[:, g * LANES:(g + 1) * LANES]
            up = pltpu.roll(t, HEAD_DIM // 2, axis=1)
            down = pltpu.roll(t, LANES - HEAD_DIM // 2, axis=1)
            rot = jnp.where(first_half, -down, up)
            lanes = slice(j * tn + g * LANES, j * tn + (g + 1) * LANES)
            o_ref[:, lanes] = (t * cos + rot * sin).astype(o_ref.dtype)


def in_proj(h, w, cos, sin, tm, tn):
    t, d = h.shape
    n = w.shape[1]
    kern = functools.partial(_in_proj_kernel, tn=tn, q_tile=2 * d // tn, k_tile=3 * d // tn)
    return pl.pallas_call(
        kern,
        grid=(t // tm,),
        in_specs=[pl.BlockSpec((tm, d), lambda i: (i, 0)),
                  pl.BlockSpec((d, n), lambda i: (0, 0), pipeline_mode=pl.Buffered(1)),
                  pl.BlockSpec((tm, LANES), lambda i: (i, 0)),
                  pl.BlockSpec((tm, LANES), lambda i: (i, 0))],
        out_specs=pl.BlockSpec((tm, n), lambda i: (i, 0)),
        out_shape=jax.ShapeDtypeStruct((t, n), bf16),
        compiler_params=_cparams("parallel"),
        name="in_proj",
    )(h, w, cos, sin)


def _rglru_kernel(gate_ref, x_ref, cw_ref, cb_ref, wbd_ref, ba_ref, bx_ref, lam_ref, o_ref,
                  xbuf, a_s, b_s, hcar, *, tc):
    c = pl.program_id(1)
    d = x_ref.shape[1]

    @pl.when(c == 0)
    def _():
        xbuf[0:SUBLANES, :] = jnp.zeros((SUBLANES, d), f32)
        hcar[...] = jnp.zeros_like(hcar)

    @pl.when(c > 0)
    def _():
        xbuf[0:SUBLANES, :] = xbuf[tc:tc + SUBLANES, :]

    xbuf[SUBLANES:SUBLANES + tc, :] = x_ref[...].astype(f32)

    u = cb_ref[...] + cw_ref[CONV_WIDTH - 1:CONV_WIDTH, :] * xbuf[SUBLANES:SUBLANES + tc, :]
    for k in range(CONV_WIDTH - 1):
        off = SUBLANES - (CONV_WIDTH - 1) + k
        u = u + cw_ref[k:k + 1, :] * xbuf[off:off + tc, :]

    ub = u.astype(bf16)
    z = -lam_ref[...]
    softplus = jnp.maximum(z, 0.0) + jnp.log(1.0 + jnp.exp(-jnp.abs(z)))
    for g in range(d // MXU_DIM):
        sl = slice(g * MXU_DIM, (g + 1) * MXU_DIM)
        zz = jnp.dot(ub[:, sl], wbd_ref[g], preferred_element_type=f32)
        r = _sigmoid_tanh(zz[:, :MXU_DIM] + ba_ref[:, sl])
        i_gate = _sigmoid_tanh(zz[:, MXU_DIM:] + bx_ref[:, sl])
        log_a = (-LRU_C) * r * softplus[:, sl]
        a = jnp.exp(log_a)
        beta = jnp.sqrt(1.0 - a * a)
        a_s[:, sl] = a
        b_s[:, sl] = beta * (i_gate * u[:, sl])

    row = lax.broadcasted_iota(jnp.int32, (SUBLANES, d), 0)

    def body(i, h):
        rows = pl.ds(pl.multiple_of(i * SUBLANES, SUBLANES), SUBLANES)
        a = a_s[rows, :]
        b = b_s[rows, :]
        shift = 1
        while shift < SUBLANES:
            keep = row >= shift
            b = a * jnp.where(keep, pltpu.roll(b, shift, axis=0), 0.0) + b
            a = a * jnp.where(keep, pltpu.roll(a, shift, axis=0), 1.0)
            shift *= 2
        h_tile = a * h + b
        b_s[rows, :] = h_tile
        return h_tile[SUBLANES - 1:SUBLANES, :]

    hcar[...] = lax.fori_loop(0, tc // SUBLANES, body, hcar[...], unroll=2)

    gt = gate_ref[...].astype(f32)
    gelu = 0.5 * gt * (1.0 + jnp.tanh(math.sqrt(2.0 / math.pi) * (gt + 0.044715 * (gt * gt * gt))))
    o_ref[...] = (b_s[...] * gelu).astype(o_ref.dtype)


def rglru(proj, cw, cb, wbd, ba, bx, lam, batch, tc):
    t = proj.shape[0]
    d = cw.shape[1]
    nc = t // batch // tc
    row = lambda b, c: (b * nc + c, 0)
    const2 = lambda b, c: (0, 0)
    return pl.pallas_call(
        functools.partial(_rglru_kernel, tc=tc),
        grid=(batch, nc),
        in_specs=[pl.BlockSpec((tc, d), lambda b, c: (b * nc + c, 0)),
                  pl.BlockSpec((tc, d), lambda b, c: (b * nc + c, 1)),
                  pl.BlockSpec((CONV_WIDTH, d), const2),
                  pl.BlockSpec((1, d), const2),
                  pl.BlockSpec(wbd.shape, lambda b, c: (0, 0, 0)),
                  pl.BlockSpec((1, d), const2),
                  pl.BlockSpec((1, d), const2),
                  pl.BlockSpec((1, d), const2)],
        out_specs=pl.BlockSpec((tc, d), row),
        out_shape=jax.ShapeDtypeStruct((t, d), bf16),
        scratch_shapes=[pltpu.VMEM((tc + SUBLANES, d), f32),
                        pltpu.VMEM((tc, d), f32),
                        pltpu.VMEM((tc, d), f32),
                        pltpu.VMEM((1, d), f32)],
        compiler_params=_cparams("parallel", "arbitrary"),
        name="rglru",
    )(proj, proj, cw, cb, wbd, ba, bx, lam)


def _diff_attn_kernel(q_ref, k_ref, v_ref, dl_ref, gs_ref, o_ref, vt, qt, s_a, s_b, m_s, l_s, acc,
                      *, tq, lambda_init):
    p = pl.program_id(2)
    n_kv = k_ref.shape[0] // tq
    q_tiles = (p, n_kv - 1 - p)

    @pl.when(p == 0)
    def _():
        for c in range(n_kv):
            vt[c] = v_ref[c * tq:(c + 1) * tq, :].astype(f32).T.astype(bf16)

    def rows_of(tile):
        return pl.ds(pl.multiple_of(tile * tq, tq), tq)

    for w in range(2):
        q_t = q_ref[rows_of(q_tiles[w]), :].astype(f32).T
        row = lax.broadcasted_iota(jnp.int32, q_t.shape, 0)
        qt[w, :, 0:tq] = jnp.where(row < HEAD_DIM, q_t, 0.0).astype(bf16)
        qt[w, :, tq:2 * tq] = jnp.where(row >= HEAD_DIM, q_t, 0.0).astype(bf16)

    m_s[...] = jnp.full_like(m_s, NEG_BIG)
    l_s[...] = jnp.zeros_like(l_s)
    acc[...] = jnp.zeros_like(acc)

    items = [(0, q_tiles[0], True), (1, q_tiles[1], True)]
    for idx in range(n_kv - 1):
        in_first = idx < q_tiles[0]
        items.append((jnp.where(in_first, 0, 1), jnp.where(in_first, idx, idx - q_tiles[0]), False))
    bufs = (s_a, s_b)

    def scores(item, s_ref):
        w, j, _ = item
        s_ref[...] = jnp.dot(k_ref[rows_of(j), :], qt[w], preferred_element_type=f32)

    def consume(item, s_ref):
        w, j, masked = item
        s_t = s_ref[...]
        if masked:
            key = lax.broadcasted_iota(jnp.int32, s_t.shape, 0)
            query = lax.broadcasted_iota(jnp.int32, s_t.shape, 1) % tq
            s_t = jnp.where(query >= key, s_t, NEG_BIG)
        m_old = m_s[w]
        m_new = jnp.maximum(m_old, jnp.max(s_t, axis=0, keepdims=True))
        alpha = jnp.exp2(m_old - m_new)
        p_t = jnp.exp2(s_t - m_new)
        l_s[w] = alpha * l_s[w] + jnp.sum(p_t, axis=0, keepdims=True)
        acc[w] = alpha * acc[w] + jnp.dot(vt[j], p_t.astype(bf16), preferred_element_type=f32)
        m_s[w] = m_new

    scores(items[0], bufs[0])
    for i, item in enumerate(items):
        if i + 1 < len(items):
            scores(items[i + 1], bufs[(i + 1) % 2])
        consume(item, bufs[i % 2])

    dl = dl_ref[...]
    lam = (jnp.exp(jnp.sum(dl[0:1, :] * dl[1:2, :], axis=-1, keepdims=True))
           - jnp.exp(jnp.sum(dl[2:3, :] * dl[3:4, :], axis=-1, keepdims=True)) + lambda_init)
    for w in range(2):
        o_all = acc[w] * (1.0 / l_s[w])
        o_t = o_all[:, 0:tq] - lam * o_all[:, tq:2 * tq]
        o_t = o_t * lax.rsqrt(jnp.mean(o_t * o_t, axis=0, keepdims=True) + EPS)
        o_ref[rows_of(q_tiles[w]), :] = (o_t.T * gs_ref[...] * (1.0 - lambda_init)).astype(o_ref.dtype)


def diff_attn(proj, dl, gs, batch, tq, lambda_init):
    t = proj.shape[0]
    seq = t // batch
    nq = seq // tq
    assert nq % 2 == 0
    d = N_HEADS * V_DIM
    q_col, k_col, v_col = 2 * N_HEADS, 3 * N_HEADS, 4 * N_HEADS
    kern = functools.partial(_diff_attn_kernel, tq=tq, lambda_init=lambda_init)
    return pl.pallas_call(
        kern,
        grid=(batch, N_HEADS, nq // 2),
        in_specs=[pl.BlockSpec((seq, V_DIM), lambda b, h, i: (b, q_col + h)),
                  pl.BlockSpec((seq, V_DIM), lambda b, h, i: (b, k_col + h)),
                  pl.BlockSpec((seq, V_DIM), lambda b, h, i: (b, v_col + h)),
                  pl.BlockSpec((4, HEAD_DIM), lambda b, h, i: (0, 0)),
                  pl.BlockSpec((1, V_DIM), lambda b, h, i: (0, 0))],
        out_specs=pl.BlockSpec((seq, V_DIM), lambda b, h, i: (b, h)),
        out_shape=jax.ShapeDtypeStruct((t, d), bf16),
        scratch_shapes=[pltpu.VMEM((nq, V_DIM, tq), bf16),
                        pltpu.VMEM((2, V_DIM, 2 * tq), bf16),
                        pltpu.VMEM((tq, 2 * tq), f32),
                        pltpu.VMEM((tq, 2 * tq), f32),
                        pltpu.VMEM((2, 1, 2 * tq), f32),
                        pltpu.VMEM((2, 1, 2 * tq), f32),
                        pltpu.VMEM((2, V_DIM, 2 * tq), f32)],
        compiler_params=_cparams("parallel", "parallel", "arbitrary"),
        name="diff_attn",
    )(proj, proj, proj, dl, gs)


def _merge_kernel(ya_ref, yb_ref, ga_ref, gb_ref, x_ref, wa_ref, wb_ref, wo_ref, g1_ref, g2_ref,
                  xo_ref, ho_ref):
    pa = jnp.dot(ya_ref[...], wa_ref[...], preferred_element_type=f32)
    pb = jnp.dot(yb_ref[...], wb_ref[...], preferred_element_type=f32)
    merged = _sigmoid(ga_ref[...].astype(f32)) * pa + _sigmoid(gb_ref[...].astype(f32)) * pb
    o = jnp.dot(merged.astype(bf16), wo_ref[...], preferred_element_type=f32)
    x_new = x_ref[...] + _rms(o, g1_ref[...])
    xo_ref[...] = x_new
    ho_ref[...] = _rms(x_new, g2_ref[...]).astype(ho_ref.dtype)


def merge(ya, yb, proj, x, wa, wb, wo, g1, g2, tm, h_dtype):
    t, d = x.shape
    ga_col = proj.shape[1] // d - 2
    row = lambda i: (i, 0)
    const = lambda i: (0, 0)
    return pl.pallas_call(
        _merge_kernel,
        grid=(t // tm,),
        in_specs=[pl.BlockSpec((tm, d), row), pl.BlockSpec((tm, d), row),
                  pl.BlockSpec((tm, d), lambda i: (i, ga_col)),
                  pl.BlockSpec((tm, d), lambda i: (i, ga_col + 1)),
                  pl.BlockSpec((tm, d), row),
                  pl.BlockSpec((d, d), const), pl.BlockSpec((d, d), const), pl.BlockSpec((d, d), const),
                  pl.BlockSpec((1, d), const), pl.BlockSpec((1, d), const)],
        out_specs=[pl.BlockSpec((tm, d), row), pl.BlockSpec((tm, d), row)],
        out_shape=[jax.ShapeDtypeStruct((t, d), f32), jax.ShapeDtypeStruct((t, d), h_dtype)],
        compiler_params=_cparams("parallel"),
        name="merge",
    )(ya, yb, proj, proj, x, wa, wb, wo, g1, g2)


def _swiglu(h, wg_ref, wu_ref, wd_ref, act, tf):
    for c in range(act.shape[1] // tf):
        sl = slice(c * tf, (c + 1) * tf)
        gate = jnp.dot(h, wg_ref[:, sl], preferred_element_type=f32)
        up = jnp.dot(h, wu_ref[:, sl], preferred_element_type=f32)
        act[:, sl] = (gate * _sigmoid(gate) * up).astype(bf16)
    return jnp.dot(act[...], wd_ref[...], preferred_element_type=f32)


def _ffn_kernel(h_ref, x_ref, wg_ref, wu_ref, wd_ref, g3_ref, g4_ref, xo_ref, ho_ref, act, *, tf):
    out = _swiglu(h_ref[...], wg_ref, wu_ref, wd_ref, act, tf)
    x_new = x_ref[...] + _rms(out, g3_ref[...])
    xo_ref[...] = x_new
    ho_ref[...] = _rms(x_new, g4_ref[...]).astype(ho_ref.dtype)


def ffn(h, x, wg, wu, wd, g3, g4, tm, tf):
    t, d = x.shape
    dff = wg.shape[1]
    row = lambda i: (i, 0)
    const = lambda i: (0, 0)
    once = pl.Buffered(1)
    return pl.pallas_call(
        functools.partial(_ffn_kernel, tf=tf),
        grid=(t // tm,),
        in_specs=[pl.BlockSpec((tm, d), row), pl.BlockSpec((tm, d), row),
                  pl.BlockSpec((d, dff), const, pipeline_mode=once),
                  pl.BlockSpec((d, dff), const, pipeline_mode=once),
                  pl.BlockSpec((dff, d), const, pipeline_mode=once),
                  pl.BlockSpec((1, d), const), pl.BlockSpec((1, d), const)],
        out_specs=[pl.BlockSpec((tm, d), row), pl.BlockSpec((tm, d), row)],
        out_shape=[jax.ShapeDtypeStruct((t, d), f32), jax.ShapeDtypeStruct((t, d), bf16)],
        scratch_shapes=[pltpu.VMEM((tm, dff), bf16)],
        compiler_params=_cparams("parallel"),
        name="ffn",
    )(h, x, wg, wu, wd, g3, g4)


def _router_kernel(h_ref, rw_ref, idx_ref, w_ref, cnt_ref, tri, cnt):
    tm = h_ref.shape[0]

    @pl.when(pl.program_id(0) == 0)
    def _():
        before = lax.broadcasted_iota(jnp.int32, (tm, tm), 0) < lax.broadcasted_iota(jnp.int32, (tm, tm), 1)
        tri[...] = jnp.where(before, 1.0, 0.0).astype(bf16)
        cnt[...] = jnp.zeros_like(cnt)

    logits = lax.dot_general(rw_ref[...], h_ref[...].astype(bf16), (((1,), (1,)), ((), ())),
                             preferred_element_type=f32)
    e_idx = lax.broadcasted_iota(jnp.int32, logits.shape, 0)
    n_e = logits.shape[0]
    v1 = jnp.max(logits, axis=0, keepdims=True)
    i1 = jnp.min(jnp.where(logits == v1, e_idx, n_e), axis=0, keepdims=True)
    rest = jnp.where(e_idx == i1, -jnp.inf, logits)
    v2 = jnp.max(rest, axis=0, keepdims=True)
    i2 = jnp.min(jnp.where(rest == v2, e_idx, n_e), axis=0, keepdims=True)
    ex = jnp.exp(v2 - v1)
    w_ref[0:1, :] = 1.0 / (1.0 + ex)
    w_ref[1:2, :] = ex / (1.0 + ex)

    sel1 = e_idx == i1
    sel2 = e_idx == i2
    sel = jnp.where(jnp.logical_or(sel1, sel2), 1.0, 0.0)
    prefix = jnp.dot(sel.astype(bf16), tri[...], preferred_element_type=f32)
    rank = prefix + cnt[:, 0:1]
    idx_ref[0:1, :] = i1
    idx_ref[1:2, :] = i2
    idx_ref[2:3, :] = jnp.sum(jnp.where(sel1, rank, 0.0), axis=0, keepdims=True).astype(jnp.int32)
    idx_ref[3:4, :] = jnp.sum(jnp.where(sel2, rank, 0.0), axis=0, keepdims=True).astype(jnp.int32)
    cnt[...] = cnt[...] + jnp.sum(sel, axis=1, keepdims=True)
    cnt_ref[...] = cnt[...]


def router(h, rw_t, tm):
    t, d = h.shape
    n_e = rw_t.shape[0]
    return pl.pallas_call(
        _router_kernel,
        grid=(t // tm,),
        in_specs=[pl.BlockSpec((tm, d), lambda i: (i, 0)),
                  pl.BlockSpec((n_e, d), lambda i: (0, 0))],
        out_specs=[pl.BlockSpec((4, tm), lambda i: (0, i)),
                   pl.BlockSpec((2, tm), lambda i: (0, i)),
                   pl.BlockSpec((n_e, LANES), lambda i: (0, 0))],
        out_shape=[jax.ShapeDtypeStruct((4, t), jnp.int32),
                   jax.ShapeDtypeStruct((2, t), f32),
                   jax.ShapeDtypeStruct((n_e, LANES), f32)],
        scratch_shapes=[pltpu.VMEM((tm, tm), bf16), pltpu.VMEM((n_e, LANES), f32)],
        compiler_params=_cparams("arbitrary"),
        name="router",
    )(h, rw_t)


def _dispatch_kernel(slot_ref, h_ref, xs_in, xs_hbm, sem):
    del xs_in
    tc = h_ref.shape[0]

    def body(t, carry):
        for k in range(TOP_K):
            pltpu.make_async_copy(h_ref.at[pl.ds(t, 1)], xs_hbm.at[pl.ds(slot_ref[k, t], 1)], sem).start()
        return carry

    lax.fori_loop(0, tc, body, 0, unroll=8)
    for k in range(TOP_K):
        pltpu.make_async_copy(h_ref, xs_hbm.at[pl.ds(0, tc)], sem).wait()


def dispatch(slots, h, n_slots, tc):
    t, d = h.shape
    return pl.pallas_call(
        _dispatch_kernel,
        grid=(t // tc,),
        in_specs=[pl.BlockSpec((TOP_K, tc), lambda i: (0, i), memory_space=pltpu.SMEM),
                  pl.BlockSpec((tc, d), lambda i: (i, 0)),
                  pl.BlockSpec(memory_space=pl.ANY)],
        out_specs=pl.BlockSpec(memory_space=pl.ANY),
        out_shape=jax.ShapeDtypeStruct((n_slots, d), h.dtype),
        scratch_shapes=[pltpu.SemaphoreType.DMA(())],
        input_output_aliases={2: 0},
        compiler_params=_cparams("arbitrary"),
        name="dispatch",
    )(slots, h, jnp.zeros((n_slots, d), h.dtype))


def _experts_kernel(te_ref, rt_ref, nu_ref, xs_ref, wg_ref, wu_ref, wd_ref, ys_ref, act, *, tf):
    del te_ref, rt_ref

    @pl.when(pl.program_id(0) < nu_ref[0])
    def _():
        ys_ref[...] = _swiglu(xs_ref[...].astype(bf16), wg_ref.at[0], wu_ref.at[0], wd_ref.at[0], act, tf)

    @pl.when(pl.program_id(0) >= nu_ref[0])
    def _():
        ys_ref[...] = jnp.zeros_like(ys_ref)


def experts(tile_expert, row_tile, n_used, xs, wg, wu, wd, te, tf):
    n_slots, d = xs.shape
    dff = wg.shape[2]
    once = pl.Buffered(1)
    return pl.pallas_call(
        functools.partial(_experts_kernel, tf=tf),
        grid_spec=pltpu.PrefetchScalarGridSpec(
            num_scalar_prefetch=3,
            grid=(n_slots // te,),
            in_specs=[pl.BlockSpec((te, d), lambda i, et, rt, nu: (rt[i], 0)),
                      pl.BlockSpec((1, d, dff), lambda i, et, rt, nu: (et[i], 0, 0), pipeline_mode=once),
                      pl.BlockSpec((1, d, dff), lambda i, et, rt, nu: (et[i], 0, 0), pipeline_mode=once),
                      pl.BlockSpec((1, dff, d), lambda i, et, rt, nu: (et[i], 0, 0), pipeline_mode=once)],
            out_specs=pl.BlockSpec((te, d), lambda i, et, rt, nu: (i, 0)),
            scratch_shapes=[pltpu.VMEM((te, dff), bf16)]),
        out_shape=jax.ShapeDtypeStruct((n_slots, d), f32),
        compiler_params=_cparams("arbitrary"),
        name="experts",
    )(tile_expert, row_tile, n_used, xs, wg, wu, wd)


def _combine_kernel(slot_ref, ys_hbm, w_ref, x_ref, g3_ref, g4_ref, xo_ref, ho_ref, buf, sem):
    tm = x_ref.shape[0]

    def body(t, carry):
        for k in range(TOP_K):
            pltpu.make_async_copy(ys_hbm.at[pl.ds(slot_ref[k, t], 1)], buf.at[k, pl.ds(t, 1)], sem).start()
        return carry

    lax.fori_loop(0, tm, body, 0, unroll=8)
    for k in range(TOP_K):
        pltpu.make_async_copy(ys_hbm.at[pl.ds(0, tm)], buf.at[k], sem).wait()

    w = w_ref[...]
    mixed = w[:, 0:1] * buf[0] + w[:, 1:2] * buf[1]
    x_new = x_ref[...] + _rms(mixed, g3_ref[...])
    xo_ref[...] = x_new
    ho_ref[...] = _rms(x_new, g4_ref[...]).astype(ho_ref.dtype)


def combine(slots, ys, w_cols, x, g3, g4, tm):
    t, d = x.shape
    row = lambda i: (i, 0)
    const = lambda i: (0, 0)
    return pl.pallas_call(
        _combine_kernel,
        grid=(t // tm,),
        in_specs=[pl.BlockSpec((TOP_K, tm), lambda i: (0, i), memory_space=pltpu.SMEM),
                  pl.BlockSpec(memory_space=pl.ANY),
                  pl.BlockSpec((tm, TOP_K), row), pl.BlockSpec((tm, d), row),
                  pl.BlockSpec((1, d), const), pl.BlockSpec((1, d), const)],
        out_specs=[pl.BlockSpec((tm, d), row), pl.BlockSpec((tm, d), row)],
        out_shape=[jax.ShapeDtypeStruct((t, d), f32), jax.ShapeDtypeStruct((t, d), bf16)],
        scratch_shapes=[pltpu.VMEM((TOP_K, tm, d), f32), pltpu.SemaphoreType.DMA(())],
        compiler_params=_cparams("arbitrary"),
        name="combine",
    )(slots, ys, w_cols, x, g3, g4)


def moe_routed(h, x, rw_t, wg, wu, wd, g3, g4, tm, te, tf):
    t, d = h.shape
    n_e = wg.shape[0]
    idx, w_rows, cnt = router(h, rw_t, tm)
    counts = cnt[:, 0].astype(jnp.int32)
    padded = (counts + te - 1) // te * te
    ends = jnp.cumsum(padded)
    offs = ends - padded
    slots = jnp.stack([offs[idx[0]] + idx[2], offs[idx[1]] + idx[3]])
    n_tiles = TOP_K * t // te + n_e
    n_used = ends[-1] // te
    tiles = jnp.minimum(jnp.arange(n_tiles, dtype=jnp.int32), n_used - 1)
    tile_expert = jnp.sum((tiles[:, None] >= (ends // te)[None, :]).astype(jnp.int32), axis=1)
    xs = dispatch(slots, h, n_tiles * te, tm)
    ys = experts(tile_expert, tiles, n_used.reshape(1).astype(jnp.int32), xs, wg, wu, wd, te, tf)
    return combine(slots, ys, w_rows.T, x, g3, g4, min(tm, 512))


def _ple_kernel(h_ref, p_ref, x_ref, wg_ref, wp_ref, g5_ref, gn_ref, xo_ref, ho_ref):
    gate = _sigmoid(jnp.dot(h_ref[...], wg_ref[...], preferred_element_type=f32))
    emb = jnp.dot(p_ref[...].astype(bf16), wp_ref[...], preferred_element_type=f32)
    x_new = x_ref[...] + _rms(gate * emb, g5_ref[...])
    xo_ref[...] = x_new
    ho_ref[...] = _rms(x_new, gn_ref[...]).astype(ho_ref.dtype)


def ple(h, p, x, wg, wp, g5, gn, tm):
    t, d = x.shape
    dp = p.shape[1]
    row = lambda i: (i, 0)
    const = lambda i: (0, 0)
    return pl.pallas_call(
        _ple_kernel,
        grid=(t // tm,),
        in_specs=[pl.BlockSpec((tm, d), row), pl.BlockSpec((tm, dp), row), pl.BlockSpec((tm, d), row),
                  pl.BlockSpec((d, d), const), pl.BlockSpec((dp, d), const),
                  pl.BlockSpec((1, d), const), pl.BlockSpec((1, d), const)],
        out_specs=[pl.BlockSpec((tm, d), row), pl.BlockSpec((tm, d), row)],
        out_shape=[jax.ShapeDtypeStruct((t, d), f32), jax.ShapeDtypeStruct((t, d), bf16)],
        compiler_params=_cparams("parallel"),
        name="ple",
    )(h, p, x, wg, wp, g5, gn)


def _gate_weights(wa, wx):
    per = MXU_DIM // wa.shape[1]

    def tiles(w):
        groups = w.reshape(w.shape[0] // per, per, w.shape[1], w.shape[2])
        eye = jnp.eye(per, dtype=w.dtype)
        return jnp.einsum('gpcd,pq->gpcqd', groups, eye).reshape(groups.shape[0], MXU_DIM, MXU_DIM)

    return jnp.concatenate([tiles(wa), tiles(wx)], axis=-1).astype(bf16)


def kernel(x, p, positions, norm_g, w_in, conv_w, conv_b, lru_wa, lru_ba, lru_wx, lru_bx, lru_lam,
           diff_lambda, subln_g, w_proj_a, w_proj_b, w_out, ffn_w_gate, ffn_w_up, ffn_w_down, router_w,
           moe_w_gate, moe_w_up, moe_w_down, ple_w_proj, ple_w_gate):
    batch, seq, d = x.shape
    depth = w_in.shape[0]
    t = batch * seq
    tm = min(1024, t)
    tq = min(512, seq)
    te = 512
    tf =512 if ffn_w_gate.shape[-1] % 512 == 0 else ffn_w_gate.shape[-1]

    xf = x.reshape(t, d)
    pos_f = positions.reshape(t, 1).astype(f32)
    inv_freq = ROPE_THETA ** (-jnp.arange(0, HEAD_DIM, 2, dtype=f32) / HEAD_DIM)
    inv_row = jnp.tile(inv_freq, LANES // (HEAD_DIM // 2)).reshape(1, LANES)
    cos, sin = rope_tables(pos_f, inv_row, tm)
    row = lambda v: v.reshape(1, -1)

    h = norm_cast(xf, row(norm_g[0, 0]), tm)
    for i in range(depth):
        lambda_init = 0.8 - 0.6 * math.exp(-0.3 * i)
        proj = in_proj(h, w_in[i].astype(bf16), cos, sin, min(tm, 512), d)
        ya = rglru(proj, conv_w[i], row(conv_b[i]), _gate_weights(lru_wa[i], lru_wx[i]),
                   row(lru_ba[i]), row(lru_bx[i]), row(lru_lam[i]), batch, tq)
        yb = diff_attn(proj, diff_lambda[i], row(subln_g[i]), batch, tq, lambda_init)
        dense = i % 2 == 0
        xf, h = merge(ya, yb, proj, xf, w_proj_a[i].astype(bf16), w_proj_b[i].astype(bf16),
                      w_out[i].astype(bf16), row(norm_g[i, 1]), row(norm_g[i, 2]), tm, bf16 if dense else f32)
        if dense:
            xf, h = ffn(h, xf, ffn_w_gate[i // 2].astype(bf16), ffn_w_up[i // 2].astype(bf16),
                        ffn_w_down[i // 2].astype(bf16), row(norm_g[i, 3]), row(norm_g[i, 4]), min(tm, 512), tf)
        else:
            xf, h = moe_routed(h, xf, router_w[i // 2].T.astype(bf16), moe_w_gate[i // 2].astype(bf16),
                               moe_w_up[i // 2].astype(bf16), moe_w_down[i // 2].astype(bf16),
                               row(norm_g[i, 3]), row(norm_g[i, 4]), tm, te, tf)
        g_next = norm_g[i + 1, 0] if i + 1 < depth else norm_g[i, 0]
        xf, h = ple(h, p[i].reshape(t, -1), xf, ple_w_gate[i].astype(bf16), ple_w_proj[i].astype(bf16),
                    row(norm_g[i, 5]), row(g_next), tm)
    return xf.reshape(batch, seq, d)
```

```python
import functools
import math

import jax
import jax.numpy as jnp
from jax import lax
from jax.experimental import pallas as pl
from jax.experimental.pallas import tpu as pltpu

EPS = 1e-6
CONV_WIDTH = 4
LRU_C = 8.0
N_RNN_BLOCKS = 16
N_HEADS = 8
HEAD_DIM = 64
V_DIM = 2 * HEAD_DIM
ROPE_THETA = 10000.0
TOP_K = 2
LANES = 128
SUBLANES = 8
MXU_DIM = 256
VMEM_LIMIT = 56 * 1024 * 1024
NEG_BIG = -1e30
ATTN_SCALE = HEAD_DIM ** -0.5 * math.log2(math.e)

f32 = jnp.float32
bf16 = jnp.bfloat16


def _cparams(*sem):
    return pltpu.CompilerParams(dimension_semantics=sem, vmem_limit_bytes=VMEM_LIMIT)


def _rms(x, g):
    return x * lax.rsqrt(jnp.mean(x * x, axis=-1, keepdims=True) + EPS) * g


def _sigmoid(x):
    return 1.0 / (1.0 + jnp.exp(-x))


def _sigmoid_tanh(x):
    return 0.5 * (1.0 + jnp.tanh(0.5 * x))


def _rope_tab_kernel(pos_ref, inv_ref, cos_ref, sin_ref):
    ang = pos_ref[...] * inv_ref[...]
    cos_ref[...] = jnp.cos(ang)
    sin_ref[...] = jnp.sin(ang)


def rope_tables(pos_f, inv_row, tm):
    t = pos_f.shape[0]
    return pl.pallas_call(
        _rope_tab_kernel,
        grid=(t // tm,),
        in_specs=[pl.BlockSpec((tm, 1), lambda i: (i, 0)),
                  pl.BlockSpec((1, LANES), lambda i: (0, 0))],
        out_specs=[pl.BlockSpec((tm, LANES), lambda i: (i, 0))] * 2,
        out_shape=[jax.ShapeDtypeStruct((t, LANES), f32)] * 2,
        compiler_params=_cparams("parallel"),
        name="rope_tables",
    )(pos_f, inv_row)


def _norm_kernel(x_ref, g_ref, o_ref):
    o_ref[...] = _rms(x_ref[...], g_ref[...]).astype(o_ref.dtype)


def norm_cast(x, g, tm):
    t, d = x.shape
    return pl.pallas_call(
        _norm_kernel,
        grid=(t // tm,),
        in_specs=[pl.BlockSpec((tm, d), lambda i: (i, 0)),
                  pl.BlockSpec((1, d), lambda i: (0, 0))],
        out_specs=pl.BlockSpec((tm, d), lambda i: (i, 0)),
        out_shape=jax.ShapeDtypeStruct((t, d), bf16),
        compiler_params=_cparams("parallel"),
        name="norm_cast",
    )(x, g)


def _in_proj_kernel(h_ref, w_ref, cos_ref, sin_ref, o_ref, *, tn, q_tile, k_tile):
    h = h_ref[...]
    for j in range(w_ref.shape[1] // tn):
        cols = slice(j * tn, (j + 1) * tn)
        acc = jnp.dot(h, w_ref[:, cols], preferred_element_type=f32)
        if j not in (q_tile, k_tile):
            o_ref[:, cols] = acc.astype(o_ref.dtype)
            continue
        scale = ATTN_SCALE if j == q_tile else 1.0
        cos = cos_ref[...] * scale
        sin = sin_ref[...] * scale
        lane = lax.broadcasted_iota(jnp.int32, cos.shape, 1)
        first_half = (lane % HEAD_DIM) < (HEAD_DIM // 2)
        for g in range(tn // LANES):
            t = acc[:, g * LANES:(g + 1) * LANES]
            up = pltpu.roll(t, HEAD_DIM // 2, axis=1)
            down = pltpu.roll(t, LANES - HEAD_DIM // 2, axis=1)
            rot = jnp.where(first_half, -down, up)
            lanes = slice(j * tn + g * LANES, j * tn + (g + 1) * LANES)
            o_ref[:, lanes] = (t * cos + rot * sin).astype(o_ref.dtype)


def in_proj(h, w, cos, sin, tm, tn):
    t, d = h.shape
    n = w.shape[1]
    kern = functools.partial(_in_proj_kernel, tn=tn, q_tile=2 * d // tn, k_tile=3 * d // tn)
    return pl.pallas_call(
        kern,
        grid=(t // tm,),
        in_specs=[pl.BlockSpec((tm, d), lambda i: (i, 0)),
                  pl.BlockSpec((d, n), lambda i: (0, 0), pipeline_mode=pl.Buffered(1)),
                  pl.BlockSpec((tm, LANES), lambda i: (i, 0)),
                  pl.BlockSpec((tm, LANES), lambda i: (i, 0))],
        out_specs=pl.BlockSpec((tm, n), lambda i: (i, 0)),
        out_shape=jax.ShapeDtypeStruct((t, n), bf16),
        compiler_params=_cparams("parallel"),
        name="in_proj",
    )(h, w, cos, sin)


def _rglru_kernel(gate_ref, x_ref, cw_ref, cb_ref, wbd_ref, ba_ref, bx_ref, lam_ref, o_ref,
                  xbuf, a_s, b_s, hcar, *, tc):
    c = pl.program_id(1)
    d = x_ref.shape[1]

    @pl.when(c == 0)
    def _():
        xbuf[0:SUBLANES, :] = jnp.zeros((SUBLANES, d), f32)
        hcar[...] = jnp.zeros_like(hcar)

    @pl.when(c > 0)
    def _():
        xbuf[0:SUBLANES, :] = xbuf[tc:tc + SUBLANES, :]

    xbuf[SUBLANES:SUBLANES + tc, :] = x_ref[...].astype(f32)

    u = cb_ref[...] + cw_ref[CONV_WIDTH - 1:CONV_WIDTH, :] * xbuf[SUBLANES:SUBLANES + tc, :]
    for k in range(CONV_WIDTH - 1):
        off = SUBLANES - (CONV_WIDTH - 1) + k
        u = u + cw_ref[k:k + 1, :] * xbuf[off:off + tc, :]

    ub = u.astype(bf16)
    z = -lam_ref[...]
    softplus = jnp.maximum(z, 0.0) + jnp.log(1.0 + jnp.exp(-jnp.abs(z)))
    for g in range(d // MXU_DIM):
        sl = slice(g * MXU_DIM, (g + 1) * MXU_DIM)
        zz = jnp.dot(ub[:, sl], wbd_ref[g], preferred_element_type=f32)
        r = _sigmoid_tanh(zz[:, :MXU_DIM] + ba_ref[:, sl])
        i_gate = _sigmoid_tanh(zz[:, MXU_DIM:] + bx_ref[:, sl])
        log_a = (-LRU_C) * r * softplus[:, sl]
        a = jnp.exp(log_a)
        beta = jnp.sqrt(1.0 - a * a)
        a_s[:, sl] = a
        b_s[:, sl] = beta * (i_gate * u[:, sl])

    row = lax.broadcasted_iota(jnp.int32, (SUBLANES, d), 0)

    def body(i, h):
        rows = pl.ds(pl.multiple_of(i * SUBLANES, SUBLANES), SUBLANES)
        a = a_s[rows, :]
        b = b_s[rows, :]
        shift = 1
        while shift < SUBLANES:
            keep = row >= shift
            b = a * jnp.where(keep, pltpu.roll(b, shift, axis=0), 0.0) + b
            a = a * jnp.where(keep, pltpu.roll(a, shift, axis=0), 1.0)
            shift *= 2
        h_tile = a * h + b
        b_s[rows, :] = h_tile
        return h_tile[SUBLANES - 1:SUBLANES, :]

    hcar[...] = lax.fori_loop(0, tc // SUBLANES, body, hcar[...], unroll=2)

    gt = gate_ref[...].astype(f32)
    gelu = 0.5 * gt * (1.0 + jnp.tanh(math.sqrt(2.0 / math.pi) * (gt + 0.044715 * (gt * gt * gt))))
    o_ref[...] = (b_s[...] * gelu).astype(o_ref.dtype)


def rglru(proj, cw, cb, wbd, ba, bx, lam, batch, tc):
    t = proj.shape[0]
    d = cw.shape[1]
    nc = t // batch // tc
    row = lambda b, c: (b * nc + c, 0)
    const2 = lambda b, c: (0, 0)
    return pl.pallas_call(
        functools.partial(_rglru_kernel, tc=tc),
        grid=(batch, nc),
        in_specs=[pl.BlockSpec((tc, d), lambda b, c: (b * nc + c, 0)),
                  pl.BlockSpec((tc, d), lambda b, c: (b * nc + c, 1)),
                  pl.BlockSpec((CONV_WIDTH, d), const2),
                  pl.BlockSpec((1, d), const2),
                  pl.BlockSpec(wbd.shape, lambda b, c: (0, 0, 0)),
                  pl.BlockSpec((1, d), const2),
                  pl.BlockSpec((1, d), const2),
                  pl.BlockSpec((1, d), const2)],
        out_specs=pl.BlockSpec((tc, d), row),
        out_shape=jax.ShapeDtypeStruct((t, d), bf16),
        scratch_shapes=[pltpu.VMEM((tc + SUBLANES, d), f32),
                        pltpu.VMEM((tc, d), f32),
                        pltpu.VMEM((tc, d), f32),
                        pltpu.VMEM((1, d), f32)],
        compiler_params=_cparams("parallel", "arbitrary"),
        name="rglru",
    )(proj, proj, cw, cb, wbd, ba, bx, lam)


def _diff_attn_kernel(q_ref, k_ref, v_ref, dl_ref, gs_ref, o_ref, vt, qt, s_a, s_b, m_s, l_s, acc,
                      *, tq, lambda_init):
    p = pl.program_id(2)
    n_kv = k_ref.shape[0] // tq
    q_tiles = (p, n_kv - 1 - p)

    @pl.when(p == 0)
    def _():
        for c in range(n_kv):
            vt[c] = v_ref[c * tq:(c + 1) * tq, :].astype(f32).T.astype(bf16)

    def rows_of(tile):
        return pl.ds(pl.multiple_of(tile * tq, tq), tq)

    for w in range(2):
        q_t = q_ref[rows_of(q_tiles[w]), :].astype(f32).T
        row = lax.broadcasted_iota(jnp.int32, q_t.shape, 0)
        qt[w, :, 0:tq] = jnp.where(row < HEAD_DIM, q_t, 0.0).astype(bf16)
        qt[w, :, tq:2 * tq] = jnp.where(row >= HEAD_DIM, q_t, 0.0).astype(bf16)

    m_s[...] = jnp.full_like(m_s, NEG_BIG)
    l_s[...] = jnp.zeros_like(l_s)
    acc[...] = jnp.zeros_like(acc)

    items = [(0, q_tiles[0], True), (1, q_tiles[1], True)]
    for idx in range(n_kv - 1):
        in_first = idx < q_tiles[0]
        items.append((jnp.where(in_first, 0, 1), jnp.where(in_first, idx, idx - q_tiles[0]), False))
    bufs = (s_a, s_b)

    def scores(item, s_ref):
        w, j, _ = item
        s_ref[...] = jnp.dot(k_ref[rows_of(j), :], qt[w], preferred_element_type=f32)

    def consume(item, s_ref):
        w, j, masked = item
        s_t = s_ref[...]
        if masked:
            key = lax.broadcasted_iota(jnp.int32, s_t.shape, 0)
            query = lax.broadcasted_iota(jnp.int32, s_t.shape, 1) % tq
            s_t = jnp.where(query >= key, s_t, NEG_BIG)
        m_old = m_s[w]
        m_new = jnp.maximum(m_old, jnp.max(s_t, axis=0, keepdims=True))
        alpha = jnp.exp2(m_old - m_new)
        p_t = jnp.exp2(s_t - m_new)
        l_s[w] = alpha * l_s[w] + jnp.sum(p_t, axis=0, keepdims=True)
        acc[w] = alpha * acc[w] + jnp.dot(vt[j], p_t.astype(bf16), preferred_element_type=f32)
        m_s[w] = m_new

    scores(items[0], bufs[0])
    for i, item in enumerate(items):
        if i + 1 < len(items):
            scores(items[i + 1], bufs[(i + 1) % 2])
        consume(item, bufs[i % 2])

    dl = dl_ref[...]
    lam = (jnp.exp(jnp.sum(dl[0:1, :] * dl[1:2, :], axis=-1, keepdims=True))
           - jnp.exp(jnp.sum(dl[2:3, :] * dl[3:4, :], axis=-1, keepdims=True)) + lambda_init)
    for w in range(2):
        o_all = acc[w] * (1.0 / l_s[w])
        o_t = o_all[:, 0:tq] - lam * o_all[:, tq:2 * tq]
        o_t = o_t * lax.rsqrt(jnp.mean(o_t * o_t, axis=0, keepdims=True) + EPS)
        o_ref[rows_of(q_tiles[w]), :] = (o_t.T * gs_ref[...] * (1.0 - lambda_init)).astype(o_ref.dtype)


def diff_attn(proj, dl, gs, batch, tq, lambda_init):
    t = proj.shape[0]
    seq = t // batch
    nq = seq // tq
    assert nq % 2 == 0
    d = N_HEADS * V_DIM
    q_col, k_col, v_col = 2 * N_HEADS, 3 * N_HEADS, 4 * N_HEADS
    kern = functools.partial(_diff_attn_kernel, tq=tq, lambda_init=lambda_init)
    return pl.pallas_call(
        kern,
        grid=(batch, N_HEADS, nq // 2),
        in_specs=[pl.BlockSpec((seq, V_DIM), lambda b, h, i: (b, q_col + h)),
                  pl.BlockSpec((seq, V_DIM), lambda b, h, i: (b, k_col + h)),
                  pl.BlockSpec((seq, V_DIM), lambda b, h, i: (b, v_col + h)),
                  pl.BlockSpec((4, HEAD_DIM), lambda b, h, i: (0, 0)),
                  pl.BlockSpec((1, V_DIM), lambda b, h, i: (0, 0))],
        out_specs=pl.BlockSpec((seq, V_DIM), lambda b, h, i: (b, h)),
        out_shape=jax.ShapeDtypeStruct((t, d), bf16),
        scratch_shapes=[pltpu.VMEM((nq, V_DIM, tq), bf16),
                        pltpu.VMEM((2, V_DIM, 2 * tq), bf16),
                        pltpu.VMEM((tq, 2 * tq), f32),
                        pltpu.VMEM((tq, 2 * tq), f32),
                        pltpu.VMEM((2, 1, 2 * tq), f32),
                        pltpu.VMEM((2, 1, 2 * tq), f32),
                        pltpu.VMEM((2, V_DIM, 2 * tq), f32)],
        compiler_params=_cparams("parallel", "parallel", "arbitrary"),
        name="diff_attn",
    )(proj, proj, proj, dl, gs)


def _merge_kernel(ya_ref, yb_ref, ga_ref, gb_ref, x_ref, wa_ref, wb_ref, wo_ref, g1_ref, g2_ref,
                  xo_ref, ho_ref):
    pa = jnp.dot(ya_ref[...], wa_ref[...], preferred_element_type=f32)
    pb = jnp.dot(yb_ref[...], wb_ref[...], preferred_element_type=f32)
    merged = _sigmoid(ga_ref[...].astype(f32)) * pa + _sigmoid(gb_ref[...].astype(f32)) * pb
    o = jnp.dot(merged.astype(bf16), wo_ref[...], preferred_element_type=f32)
    x_new = x_ref[...] + _rms(o, g1_ref[...])
    xo_ref[...] = x_new
    ho_ref[...] = _rms(x_new, g2_ref[...]).astype(ho_ref.dtype)


def merge(ya, yb, proj, x, wa, wb, wo, g1, g2, tm, h_dtype):
    t, d = x.shape
    ga_col = proj.shape[1] // d - 2
    row = lambda i: (i, 0)
    const = lambda i: (0, 0)
    return pl.pallas_call(
        _merge_kernel,
        grid=(t // tm,),
        in_specs=[pl.BlockSpec((tm, d), row), pl.BlockSpec((tm, d), row),
                  pl.BlockSpec((tm, d), lambda i: (i, ga_col)),
                  pl.BlockSpec((tm, d), lambda i: (i, ga_col + 1)),
                  pl.BlockSpec((tm, d), row),
                  pl.BlockSpec((d, d), const), pl.BlockSpec((d, d), const), pl.BlockSpec((d, d), const),
                  pl.BlockSpec((1, d), const), pl.BlockSpec((1, d), const)],
        out_specs=[pl.BlockSpec((tm, d), row), pl.BlockSpec((tm, d), row)],
        out_shape=[jax.ShapeDtypeStruct((t, d), f32), jax.ShapeDtypeStruct((t, d), h_dtype)],
        compiler_params=_cparams("parallel"),
        name="merge",
    )(ya, yb, proj, proj, x, wa, wb, wo, g1, g2)


def _mixer_tail(x, f, g3_ref, g4_ref, p_ref, pwg_ref, pwp_ref, g5_ref, gn_ref, xo_ref, ho_ref):
    x = x + _rms(f, g3_ref[...])
    gate = _sigmoid(jnp.dot(_rms(x, g4_ref[...]).astype(bf16), pwg_ref[...], preferred_element_type=f32))
    emb = jnp.dot(p_ref[...].astype(bf16), pwp_ref[...], preferred_element_type=f32)
    x = x + _rms(gate * emb, g5_ref[...])
    xo_ref[...] = x
    ho_ref[...] = _rms(x, gn_ref[...]).astype(ho_ref.dtype)


def _tail_specs(tm, d, dp):
    row = lambda i: (i, 0)
    const = lambda i: (0, 0)
    vec = pl.BlockSpec((1, d), const)
    once = pl.Buffered(1)
    in_specs = [vec, vec, pl.BlockSpec((tm, dp), row),
                pl.BlockSpec((d, d), const, pipeline_mode=once),
                pl.BlockSpec((dp, d), const, pipeline_mode=once), vec, vec]
    out_specs = [pl.BlockSpec((tm, d), row), pl.BlockSpec((tm, d), row)]
    return in_specs, out_specs


def _swiglu(h, wg_ref, wu_ref, wd_ref, act, tf):
    for c in range(act.shape[1] // tf):
        sl = slice(c * tf, (c + 1) * tf)
        gate = jnp.dot(h, wg_ref[:, sl], preferred_element_type=f32)
        up = jnp.dot(h, wu_ref[:, sl], preferred_element_type=f32)
        act[:, sl] = (gate * _sigmoid(gate) * up).astype(bf16)
    return jnp.dot(act[...], wd_ref[...], preferred_element_type=f32)


def _ffn_kernel(h_ref, x_ref, wg_ref, wu_ref, wd_ref, *rest, tf):
    *tail, act = rest
    _mixer_tail(x_ref[...], _swiglu(h_ref[...], wg_ref, wu_ref, wd_ref, act, tf), *tail)


def ffn(h, x, wg, wu, wd, tail, tm, tf):
    t, d = x.shape
    dff = wg.shape[1]
    row = lambda i: (i, 0)
    const = lambda i: (0, 0)
    once = pl.Buffered(1)
    tail_in, outs = _tail_specs(tm, d, tail[2].shape[1])
    return pl.pallas_call(
        functools.partial(_ffn_kernel, tf=tf),
        grid=(t // tm,),
        in_specs=[pl.BlockSpec((tm, d), row), pl.BlockSpec((tm, d), row),
                  pl.BlockSpec((d, dff), const, pipeline_mode=once),
                  pl.BlockSpec((d, dff), const, pipeline_mode=once),
                  pl.BlockSpec((dff, d), const, pipeline_mode=once)] + tail_in,
        out_specs=outs,
        out_shape=[jax.ShapeDtypeStruct((t, d), f32), jax.ShapeDtypeStruct((t, d), bf16)],
        scratch_shapes=[pltpu.VMEM((tm, dff), bf16)],
        compiler_params=_cparams("parallel"),
        name="ffn",
    )(h, x, wg, wu, wd, *tail)


def _router_kernel(h_ref, rw_ref, idx_ref, w_ref, cnt_ref, tri, cnt):
    tm = h_ref.shape[0]

    @pl.when(pl.program_id(0) == 0)
    def _():
        before = lax.broadcasted_iota(jnp.int32, (tm, tm), 0) < lax.broadcasted_iota(jnp.int32, (tm, tm), 1)
        tri[...] = jnp.where(before, 1.0, 0.0).astype(bf16)
        cnt[...] = jnp.zeros_like(cnt)

    logits = lax.dot_general(rw_ref[...], h_ref[...].astype(bf16), (((1,), (1,)), ((), ())),
                             preferred_element_type=f32)
    e_idx = lax.broadcasted_iota(jnp.int32, logits.shape, 0)
    n_e = logits.shape[0]
    v1 = jnp.max(logits, axis=0, keepdims=True)
    i1 = jnp.min(jnp.where(logits == v1, e_idx, n_e), axis=0, keepdims=True)
    rest = jnp.where(e_idx == i1, -jnp.inf, logits)
    v2 = jnp.max(rest, axis=0, keepdims=True)
    i2 = jnp.min(jnp.where(rest == v2, e_idx, n_e), axis=0, keepdims=True)
    ex = jnp.exp(v2 - v1)
    w_ref[0:1, :] = 1.0 / (1.0 + ex)
    w_ref[1:2, :] = ex / (1.0 + ex)

    sel1 = e_idx == i1
    sel2 = e_idx == i2
    sel = jnp.where(jnp.logical_or(sel1, sel2), 1.0, 0.0)
    prefix = jnp.dot(sel.astype(bf16), tri[...], preferred_element_type=f32)
    rank = prefix + cnt[:, 0:1]
    idx_ref[0:1, :] = i1
    idx_ref[1:2, :] = i2
    idx_ref[2:3, :] = jnp.sum(jnp.where(sel1, rank, 0.0), axis=0, keepdims=True).astype(jnp.int32)
    idx_ref[3:4, :] = jnp.sum(jnp.where(sel2, rank, 0.0), axis=0, keepdims=True).astype(jnp.int32)
    cnt[...] = cnt[...] + jnp.sum(sel, axis=1, keepdims=True)
    cnt_ref[...] = cnt[...]


def router(h, rw_t, tm):
    t, d = h.shape
    n_e = rw_t.shape[0]
    return pl.pallas_call(
        _router_kernel,
        grid=(t // tm,),
        in_specs=[pl.BlockSpec((tm, d), lambda i: (i, 0)),
                  pl.BlockSpec((n_e, d), lambda i: (0, 0))],
        out_specs=[pl.BlockSpec((4, tm), lambda i: (0, i)),
                   pl.BlockSpec((2, tm), lambda i: (0, i)),
                   pl.BlockSpec((n_e, LANES), lambda i: (0, 0))],
        out_shape=[jax.ShapeDtypeStruct((4, t), jnp.int32),
                   jax.ShapeDtypeStruct((2, t), f32),
                   jax.ShapeDtypeStruct((n_e, LANES), f32)],
        scratch_shapes=[pltpu.VMEM((tm, tm), bf16), pltpu.VMEM((n_e, LANES), f32)],
        compiler_params=_cparams("arbitrary"),
        name="router",
    )(h, rw_t)


def _dispatch_kernel(slot1_ref, slot2_ref, h_ref, xs_in, xs_hbm, sem):
    del xs_in
    tc = h_ref.shape[0]

    def body(t, carry):
        for slot_ref in (slot1_ref, slot2_ref):
            pltpu.make_async_copy(h_ref.at[pl.ds(t, 1)], xs_hbm.at[pl.ds(slot_ref[t], 1)], sem).start()
        return carry

    lax.fori_loop(0, tc, body, 0, unroll=8)
    for _ in range(TOP_K):
        pltpu.make_async_copy(h_ref, xs_hbm.at[pl.ds(0, tc)], sem).wait()


def dispatch(slots, h, n_slots, tc):
    t, d = h.shape
    smem_1d = pl.BlockSpec((tc,), lambda i: (i,), memory_space=pltpu.SMEM)
    return pl.pallas_call(
        _dispatch_kernel,
        grid=(t // tc,),
        in_specs=[smem_1d, smem_1d,
                  pl.BlockSpec((tc, d), lambda i: (i, 0)),
                  pl.BlockSpec(memory_space=pl.ANY)],
        out_specs=pl.BlockSpec(memory_space=pl.ANY),
        out_shape=jax.ShapeDtypeStruct((n_slots, d), h.dtype),
        scratch_shapes=[pltpu.SemaphoreType.DMA(())],
        input_output_aliases={3: 0},
        compiler_params=_cparams("arbitrary"),
        name="dispatch",
    )(slots[0], slots[1], h, jnp.zeros((n_slots, d), h.dtype))


def _experts_kernel(te_ref, rt_ref, nu_ref, xs_ref, wg_ref, wu_ref, wd_ref, ys_ref, act, *, tf):
    del te_ref, rt_ref

    @pl.when(pl.program_id(0) < nu_ref[0])
    def _():
        ys_ref[...] = _swiglu(xs_ref[...].astype(bf16), wg_ref.at[0], wu_ref.at[0], wd_ref.at[0], act, tf)

    @pl.when(pl.program_id(0) >= nu_ref[0])
    def _():
        ys_ref[...] = jnp.zeros_like(ys_ref)


def experts(tile_expert, row_tile, n_used, xs, wg, wu, wd, te, tf):
    n_slots, d = xs.shape
    dff = wg.shape[2]
    once = pl.Buffered(1)
    return pl.pallas_call(
        functools.partial(_experts_kernel, tf=tf),
        grid_spec=pltpu.PrefetchScalarGridSpec(
            num_scalar_prefetch=3,
            grid=(n_slots // te,),
            in_specs=[pl.BlockSpec((te, d), lambda i, et, rt, nu: (rt[i], 0)),
                      pl.BlockSpec((1, d, dff), lambda i, et, rt, nu: (et[i], 0, 0), pipeline_mode=once),
                      pl.BlockSpec((1, d, dff), lambda i, et, rt, nu: (et[i], 0, 0), pipeline_mode=once),
                      pl.BlockSpec((1, dff, d), lambda i, et, rt, nu: (et[i], 0, 0), pipeline_mode=once)],
            out_specs=pl.BlockSpec((te, d), lambda i, et, rt, nu: (i, 0)),
            scratch_shapes=[pltpu.VMEM((te, dff), bf16)]),
        out_shape=jax.ShapeDtypeStruct((n_slots, d), f32),
        compiler_params=_cparams("arbitrary"),
        name="experts",
    )(tile_expert, row_tile, n_used, xs, wg, wu, wd)


def _combine_kernel(slot1_ref, slot2_ref, ys_hbm, w_ref, x_ref, *rest):
    *tail, buf, sem = rest
    tm = x_ref.shape[0]

    def body(t, carry):
        for k, slot_ref in enumerate((slot1_ref, slot2_ref)):
            pltpu.make_async_copy(ys_hbm.at[pl.ds(slot_ref[t], 1)], buf.at[k, pl.ds(t, 1)], sem).start()
        return carry

    lax.fori_loop(0, tm, body, 0, unroll=8)
    for k in range(TOP_K):
        pltpu.make_async_copy(ys_hbm.at[pl.ds(0, tm)], buf.at[k], sem).wait()

    w = w_ref[...]
    _mixer_tail(x_ref[...], w[:, 0:1] * buf[0] + w[:, 1:2] * buf[1], *tail)


def combine(slots, ys, w_cols, x, tail, tm):
    t, d = x.shape
    row = lambda i: (i, 0)
    smem_1d = pl.BlockSpec((tm,), lambda i: (i,), memory_space=pltpu.SMEM)
    tail_in, outs = _tail_specs(tm, d, tail[2].shape[1])
    return pl.pallas_call(
        _combine_kernel,
        grid=(t // tm,),
        in_specs=[smem_1d, smem_1d,
                  pl.BlockSpec(memory_space=pl.ANY),
                  pl.BlockSpec((tm, TOP_K), row), pl.BlockSpec((tm, d), row)] + tail_in,
        out_specs=outs,
        out_shape=[jax.ShapeDtypeStruct((t, d), f32), jax.ShapeDtypeStruct((t, d), bf16)],
        scratch_shapes=[pltpu.VMEM((TOP_K, tm, d), f32), pltpu.SemaphoreType.DMA(())],
        compiler_params=_cparams("arbitrary"),
        name="combine",
    )(slots[0], slots[1], ys, w_cols, x, *tail)


def moe_routed(h, x, rw_t, wg, wu, wd, tail, tm, te, tf):
    t, d = h.shape
    n_e = wg.shape[0]
    idx, w_rows, cnt = router(h, rw_t, tm)
    counts = cnt[:, 0].astype(jnp.int32)
    padded = (counts + te - 1) // te * te
    ends = jnp.cumsum(padded)
    offs = ends - padded
    slots = jnp.stack([offs[idx[0]] + idx[2], offs[idx[1]] + idx[3]])
    n_tiles = TOP_K * t // te + n_e
    n_used = ends[-1] // te
    tiles = jnp.minimum(jnp.arange(n_tiles, dtype=jnp.int32), n_used - 1)
    tile_expert = jnp.sum((tiles[:, None] >= (ends // te)[None, :]).astype(jnp.int32), axis=1)
    xs = dispatch(slots, h, n_tiles * te, tm)
    ys = experts(tile_expert, tiles, n_used.reshape(1).astype(jnp.int32), xs, wg, wu, wd, te, tf)
    return combine(slots, ys, w_rows.T, x, tail, min(tm, 512))


def _gate_weights(wa, wx):
    per = MXU_DIM // wa.shape[1]

    def tiles(w):
        groups = w.reshape(w.shape[0] // per, per, w.shape[1], w.shape[2])
        eye = jnp.eye(per, dtype=w.dtype)
        return jnp.einsum('gpcd,pq->gpcqd', groups, eye).reshape(groups.shape[0], MXU_DIM, MXU_DIM)

    return jnp.concatenate([tiles(wa), tiles(wx)], axis=-1).astype(bf16)


def kernel(x, p, positions, norm_g, w_in, conv_w, conv_b, lru_wa, lru_ba, lru_wx, lru_bx, lru_lam,
           diff_lambda, subln_g, w_proj_a, w_proj_b, w_out, ffn_w_gate, ffn_w_up, ffn_w_down, router_w,
           moe_w_gate, moe_w_up, moe_w_down, ple_w_proj, ple_w_gate):
    batch, seq, d = x.shape
    depth = w_in.shape[0]
    t = batch * seq
    tm = min(1024, t)
    tq = min(512, seq)
    te = 512
    tf = 512 if ffn_w_gate.shape[-1] % 512 == 0 else ffn_w_gate.shape[-1]

    xf = x.reshape(t, d)
    pos_f = positions.reshape(t, 1).astype(f32)
    inv_freq = ROPE_THETA ** (-jnp.arange(0, HEAD_DIM, 2, dtype=f32) / HEAD_DIM)
    inv_row = jnp.tile(inv_freq, LANES // (HEAD_DIM // 2)).reshape(1, LANES)
    cos, sin = rope_tables(pos_f, inv_row, tm)
    row = lambda v: v.reshape(1, -1)

    h = norm_cast(xf, row(norm_g[0, 0]), tm)
    for i in range(depth):
        lambda_init = 0.8 - 0.6 * math.exp(-0.3 * i)
        proj = in_proj(h, w_in[i].astype(bf16), cos, sin, min(tm, 512), d)
        ya = rglru(proj, conv_w[i], row(conv_b[i]), _gate_weights(lru_wa[i], lru_wx[i]),
                   row(lru_ba[i]), row(lru_bx[i]), row(lru_lam[i]), batch, tq)
        yb = diff_attn(proj, diff_lambda[i], row(subln_g[i]), batch, tq, lambda_init)
        dense = i % 2 == 0
        xf, h = merge(ya, yb, proj, xf, w_proj_a[i].astype(bf16), w_proj_b[i].astype(bf16),
                      w_out[i].astype(bf16), row(norm_g[i, 1]), row(norm_g[i, 2]), tm, bf16 if dense else f32)
        g_next = norm_g[i + 1, 0] if i + 1 < depth else norm_g[i, 0]
        tail = (row(norm_g[i, 3]), row(norm_g[i, 4]), p[i].reshape(t, -1), ple_w_gate[i].astype(bf16),
                ple_w_proj[i].astype(bf16), row(norm_g[i, 5]), row(g_next))
        if dense:
            xf, h = ffn(h, xf, ffn_w_gate[i // 2].astype(bf16), ffn_w_up[i // 2].astype(bf16),
                        ffn_w_down[i // 2].astype(bf16), tail, min(tm, 512), tf)
        else:
            xf, h = moe_routed(h, xf, router_w[i // 2].T.astype(bf16), moe_w_gate[i // 2].astype(bf16),
                               moe_w_up[i // 2].astype(bf16), moe_w_down[i // 2].astype(bf16), tail, tm, te, tf)
    return xf.reshape(batch, seq, d)
```

```python
import functools
import math

import jax
import jax.numpy as jnp
from jax import lax
from jax.experimental import pallas as pl
from jax.experimental.pallas import tpu as pltpu

EPS = 1e-6
CONV_WIDTH = 4
LRU_C = 8.0
N_RNN_BLOCKS = 16
N_HEADS = 8
HEAD_DIM = 64
V_DIM = 2 * HEAD_DIM
ROPE_THETA = 10000.0
TOP_K = 2
LANES = 128
SUBLANES = 8
MXU_DIM = 256
VMEM_LIMIT = 56 * 1024 * 1024
NEG_BIG = -1e30
ATTN_SCALE = HEAD_DIM ** -0.5 * math.log2(math.e)

f32 = jnp.float32
bf16 = jnp.bfloat16


def _cparams(*sem):
    return pltpu.CompilerParams(dimension_semantics=sem, vmem_limit_bytes=VMEM_LIMIT)


def _rms(x, g):
    return x * lax.rsqrt(jnp.mean(x * x, axis=-1, keepdims=True) + EPS) * g


def _sigmoid(x):
    return 1.0 / (1.0 + jnp.exp(-x))


def _sigmoid_tanh(x):
    return 0.5 * (1.0 + jnp.tanh(0.5 * x))


def _rope_tab_kernel(pos_ref, inv_ref, cos_ref, sin_ref):
    ang = pos_ref[...] * inv_ref[...]
    cos_ref[...] = jnp.cos(ang)
    sin_ref[...] = jnp.sin(ang)


def rope_tables(pos_f, inv_row, tm):
    t = pos_f.shape[0]
    return pl.pallas_call(
        _rope_tab_kernel,
        grid=(t // tm,),
        in_specs=[pl.BlockSpec((tm, 1), lambda i: (i, 0)),
                  pl.BlockSpec((1, LANES), lambda i: (0, 0))],
        out_specs=[pl.BlockSpec((tm, LANES), lambda i: (i, 0))] * 2,
        out_shape=[jax.ShapeDtypeStruct((t, LANES), f32)] * 2,
        compiler_params=_cparams("parallel"),
        name="rope_tables",
    )(pos_f, inv_row)


def _in_proj_kernel(x_ref, g_ref, w_ref, cos_ref, sin_ref, o_ref, *, tn, q_tile, k_tile):
    h = _rms(x_ref[...], g_ref[...]).astype(bf16)
    for j in range(w_ref.shape[1] // tn):
        cols = slice(j * tn, (j + 1) * tn)
        acc = jnp.dot(h, w_ref[:, cols], preferred_element_type=f32)
        if j not in (q_tile, k_tile):
            o_ref[:, cols] = acc.astype(o_ref.dtype)
            continue
        scale = ATTN_SCALE if j == q_tile else 1.0
        cos = cos_ref[...] * scale
        sin = sin_ref[...] * scale
        lane = lax.broadcasted_iota(jnp.int32, cos.shape, 1)
        first_half = (lane % HEAD_DIM) < (HEAD_DIM // 2)
        for g in range(tn // LANES):
            t = acc[:, g * LANES:(g + 1) * LANES]
            up = pltpu.roll(t, HEAD_DIM // 2, axis=1)
            down = pltpu.roll(t, LANES - HEAD_DIM // 2, axis=1)
            rot = jnp.where(first_half, -down, up)
            lanes = slice(j * tn + g * LANES, j * tn + (g + 1) * LANES)
            o_ref[:, lanes] = (t * cos + rot * sin).astype(o_ref.dtype)


def in_proj(x, g, w, cos, sin, tm, tn):
    t, d = x.shape
    n = w.shape[1]
    kern = functools.partial(_in_proj_kernel, tn=tn, q_tile=2 * d // tn, k_tile=3 * d // tn)
    return pl.pallas_call(
        kern,
        grid=(t // tm,),
        in_specs=[pl.BlockSpec((tm, d), lambda i: (i, 0)),
                  pl.BlockSpec((1, d), lambda i: (0, 0)),
                  pl.BlockSpec((d, n), lambda i: (0, 0), pipeline_mode=pl.Buffered(1)),
                  pl.BlockSpec((tm, LANES), lambda i: (i, 0)),
                  pl.BlockSpec((tm, LANES), lambda i: (i, 0))],
        out_specs=pl.BlockSpec((tm, n), lambda i: (i, 0)),
        out_shape=jax.ShapeDtypeStruct((t, n), bf16),
        compiler_params=_cparams("parallel"),
        name="in_proj",
    )(x, g, w, cos, sin)


def _rglru_kernel(gate_ref, x_ref, cw_ref, cb_ref, wbd_ref, ba_ref, bx_ref, lam_ref, o_ref,
                  xbuf, a_s, b_s, hcar, *, tc):
    c = pl.program_id(1)
    d = x_ref.shape[1]

    @pl.when(c == 0)
    def _():
        xbuf[0:SUBLANES, :] = jnp.zeros((SUBLANES, d), f32)
        hcar[...] = jnp.zeros_like(hcar)

    @pl.when(c > 0)
    def _():
        xbuf[0:SUBLANES, :] = xbuf[tc:tc + SUBLANES, :]

    xbuf[SUBLANES:SUBLANES + tc, :] = x_ref[...].astype(f32)

    u = cb_ref[...] + cw_ref[CONV_WIDTH - 1:CONV_WIDTH, :] * xbuf[SUBLANES:SUBLANES + tc, :]
    for k in range(CONV_WIDTH - 1):
        off = SUBLANES - (CONV_WIDTH - 1) + k
        u = u + cw_ref[k:k + 1, :] * xbuf[off:off + tc, :]

    ub = u.astype(bf16)
    z = -lam_ref[...]
    softplus = jnp.maximum(z, 0.0) + jnp.log(1.0 + jnp.exp(-jnp.abs(z)))
    for g in range(d // MXU_DIM):
        sl = slice(g * MXU_DIM, (g + 1) * MXU_DIM)
        zz = jnp.dot(ub[:, sl], wbd_ref[g], preferred_element_type=f32)
        r = _sigmoid_tanh(zz[:, :MXU_DIM] + ba_ref[:, sl])
        i_gate = _sigmoid_tanh(zz[:, MXU_DIM:] + bx_ref[:, sl])
        log_a = (-LRU_C) * r * softplus[:, sl]
        a = jnp.exp(log_a)
        beta = jnp.sqrt(1.0 - a * a)
        a_s[:, sl] = a
        b_s[:, sl] = beta * (i_gate * u[:, sl])

    row = lax.broadcasted_iota(jnp.int32, (SUBLANES, d), 0)

    def body(i, h):
        rows = pl.ds(pl.multiple_of(i * SUBLANES, SUBLANES), SUBLANES)
        a = a_s[rows, :]
        b = b_s[rows, :]
        shift = 1
        while shift < SUBLANES:
            keep = row >= shift
            b = a * jnp.where(keep, pltpu.roll(b, shift, axis=0), 0.0) + b
            a = a * jnp.where(keep, pltpu.roll(a, shift, axis=0), 1.0)
            shift *= 2
        h_tile = a * h + b
        b_s[rows, :] = h_tile
        return h_tile[SUBLANES - 1:SUBLANES, :]

    hcar[...] = lax.fori_loop(0, tc // SUBLANES, body, hcar[...], unroll=2)

    gt = gate_ref[...].astype(f32)
    gelu = 0.5 * gt * (1.0 + jnp.tanh(math.sqrt(2.0 / math.pi) * (gt + 0.044715 * (gt * gt * gt))))
    o_ref[...] = (b_s[...] * gelu).astype(o_ref.dtype)


def rglru(proj, cw, cb, wbd, ba, bx, lam, batch, tc):
    t = proj.shape[0]
    d = cw.shape[1]
    nc = t // batch // tc
    row = lambda b, c: (b * nc + c, 0)
    const2 = lambda b, c: (0, 0)
    return pl.pallas_call(
        functools.partial(_rglru_kernel, tc=tc),
        grid=(batch, nc),
        in_specs=[pl.BlockSpec((tc, d), lambda b, c: (b * nc + c, 0)),
                  pl.BlockSpec((tc, d), lambda b, c: (b * nc + c, 1)),
                  pl.BlockSpec((CONV_WIDTH, d), const2),
                  pl.BlockSpec((1, d), const2),
                  pl.BlockSpec(wbd.shape, lambda b, c: (0, 0, 0)),
                  pl.BlockSpec((1, d), const2),
                  pl.BlockSpec((1, d), const2),
                  pl.BlockSpec((1, d), const2)],
        out_specs=pl.BlockSpec((tc, d), row),
        out_shape=jax.ShapeDtypeStruct((t, d), bf16),
        scratch_shapes=[pltpu.VMEM((tc + SUBLANES, d), f32),
                        pltpu.VMEM((tc, d), f32),
                        pltpu.VMEM((tc, d), f32),
                        pltpu.VMEM((1, d), f32)],
        compiler_params=_cparams("parallel", "arbitrary"),
        name="rglru",
    )(proj, proj, cw, cb, wbd, ba, bx, lam)


def _diff_attn_kernel(q_ref, k_ref, v_ref, dl_ref, gs_ref, o_ref, vt, qt, s_a, s_b, m_s, l_s, acc,
                      *, tq, lambda_init):
    n_kv = k_ref.shape[0] // tq
    for c in range(n_kv):
        vt[c] = v_ref[c * tq:(c + 1) * tq, :].astype(f32).T.astype(bf16)
    dl = dl_ref[...]
    lam = (jnp.exp(jnp.sum(dl[0:1, :] * dl[1:2, :], axis=-1, keepdims=True))
           - jnp.exp(jnp.sum(dl[2:3, :] * dl[3:4, :], axis=-1, keepdims=True)) + lambda_init)

    def rows_of(tile):
        return slice(tile * tq, (tile + 1) * tq)

    def scores(item, s_ref):
        w, j, _ = item
        s_ref[...] = jnp.dot(k_ref[rows_of(j), :], qt[w], preferred_element_type=f32)

    def consume(item, s_ref):
        w, j, masked = item
        s_t = s_ref[...]
        if masked:
            key = lax.broadcasted_iota(jnp.int32, s_t.shape, 0)
            query = lax.broadcasted_iota(jnp.int32, s_t.shape, 1) % tq
            s_t = jnp.where(query >= key, s_t, NEG_BIG)
        m_old = m_s[w]
        m_new = jnp.maximum(m_old, jnp.max(s_t, axis=0, keepdims=True))
        alpha = jnp.exp2(m_old - m_new)
        p_t = jnp.exp2(s_t - m_new)
        l_s[w] = alpha * l_s[w] + jnp.sum(p_t, axis=0, keepdims=True)
        acc[w] = alpha * acc[w] + jnp.dot(vt[j], p_t.astype(bf16), preferred_element_type=f32)
        m_s[w] = m_new

    def finalize(w, q_tile):
        o_all = acc[w] * (1.0 / l_s[w])
        o_t = o_all[:, 0:tq] - lam * o_all[:, tq:2 * tq]
        o_t = o_t * lax.rsqrt(jnp.mean(o_t * o_t, axis=0, keepdims=True) + EPS)
        o_ref[rows_of(q_tile), :] = (o_t.T * gs_ref[...] * (1.0 - lambda_init)).astype(o_ref.dtype)

    items = [(q, j, j == q) for q in range(n_kv) for j in range(q + 1)]
    bufs = (s_a, s_b)

    def start_query_tile(q):
        w = q % 2
        q_t = q_ref[rows_of(q), :].astype(f32).T
        row = lax.broadcasted_iota(jnp.int32, q_t.shape, 0)
        qt[w, :, 0:tq] = jnp.where(row < HEAD_DIM, q_t, 0.0).astype(bf16)
        qt[w, :, tq:2 * tq] = jnp.where(row >= HEAD_DIM, q_t, 0.0).astype(bf16)
        m_s[w] = jnp.full(m_s.shape[1:], NEG_BIG, f32)
        l_s[w] = jnp.zeros(l_s.shape[1:], f32)
        acc[w] = jnp.zeros(acc.shape[1:], f32)

    start_query_tile(0)
    scores((0, 0, True), bufs[0])
    for i, (q, j, masked) in enumerate(items):
        if i + 1 < len(items):
            qn, jn, mn = items[i + 1]
            if jn == 0:
                start_query_tile(qn)
            scores((qn % 2, jn, mn), bufs[(i + 1) % 2])
        consume((q % 2, j, masked), bufs[i % 2])
        if masked:
            finalize(q % 2, q)


def diff_attn(proj, dl, gs, batch, tq, lambda_init):
    t = proj.shape[0]
    seq = t // batch
    nq = seq // tq
    d = N_HEADS * V_DIM
    q_col, k_col, v_col = 2 * N_HEADS, 3 * N_HEADS, 4 * N_HEADS
    kern = functools.partial(_diff_attn_kernel, tq=tq, lambda_init=lambda_init)
    return pl.pallas_call(
        kern,
        grid=(batch, N_HEADS),
        in_specs=[pl.BlockSpec((seq, V_DIM), lambda b, h: (b, q_col + h)),
                  pl.BlockSpec((seq, V_DIM), lambda b, h: (b, k_col + h)),
                  pl.BlockSpec((seq, V_DIM), lambda b, h: (b, v_col + h)),
                  pl.BlockSpec((4, HEAD_DIM), lambda b, h: (0, 0)),
                  pl.BlockSpec((1, V_DIM), lambda b, h: (0, 0))],
        out_specs=pl.BlockSpec((seq, V_DIM), lambda b, h: (b, h)),
        out_shape=jax.ShapeDtypeStruct((t, d), bf16),
        scratch_shapes=[pltpu.VMEM((nq, V_DIM, tq), bf16),
                        pltpu.VMEM((2, V_DIM, 2 * tq), bf16),
                        pltpu.VMEM((tq, 2 * tq), f32),
                        pltpu.VMEM((tq, 2 * tq), f32),
                        pltpu.VMEM((2, 1, 2 * tq), f32),
                        pltpu.VMEM((2, 1, 2 * tq), f32),
                        pltpu.VMEM((2, V_DIM, 2 * tq), f32)],
        compiler_params=_cparams("parallel", "parallel"),
        name="diff_attn",
    )(proj, proj, proj, dl, gs)


def _merge_kernel(ya_ref, yb_ref, ga_ref, gb_ref, x_ref, wa_ref, wb_ref, wo_ref, g1_ref, g2_ref,
                  xo_ref, ho_ref):
    pa = jnp.dot(ya_ref[...], wa_ref[...], preferred_element_type=f32)
    pb = jnp.dot(yb_ref[...], wb_ref[...], preferred_element_type=f32)
    merged = _sigmoid(ga_ref[...].astype(f32)) * pa + _sigmoid(gb_ref[...].astype(f32)) * pb
    o = jnp.dot(merged.astype(bf16), wo_ref[...], preferred_element_type=f32)
    x_new = x_ref[...] + _rms(o, g1_ref[...])
    xo_ref[...] = x_new
    ho_ref[...] = _rms(x_new, g2_ref[...]).astype(ho_ref.dtype)


def merge(ya, yb, proj, x, wa, wb, wo, g1, g2, tm, h_dtype):
    t, d = x.shape
    ga_col = proj.shape[1] // d - 2
    row = lambda i: (i, 0)
    const = lambda i: (0, 0)
    return pl.pallas_call(
        _merge_kernel,
        grid=(t // tm,),
        in_specs=[pl.BlockSpec((tm, d), row), pl.BlockSpec((tm, d), row),
                  pl.BlockSpec((tm, d), lambda i: (i, ga_col)),
                  pl.BlockSpec((tm, d), lambda i: (i, ga_col + 1)),
                  pl.BlockSpec((tm, d), row),
                  pl.BlockSpec((d, d), const), pl.BlockSpec((d, d), const), pl.BlockSpec((d, d), const),
                  pl.BlockSpec((1, d), const), pl.BlockSpec((1, d), const)],
        out_specs=[pl.BlockSpec((tm, d), row), pl.BlockSpec((tm, d), row)],
        out_shape=[jax.ShapeDtypeStruct((t, d), f32), jax.ShapeDtypeStruct((t, d), h_dtype)],
        compiler_params=_cparams("parallel"),
        name="merge",
    )(ya, yb, proj, proj, x, wa, wb, wo, g1, g2)


def _mixer_tail(x, f, g3_ref, g4_ref, p_ref, pwg_ref, pwp_ref, g5_ref, xo_ref):
    x = x + _rms(f, g3_ref[...])
    gate = _sigmoid(jnp.dot(_rms(x, g4_ref[...]).astype(bf16), pwg_ref[...], preferred_element_type=f32))
    emb = jnp.dot(p_ref[...].astype(bf16), pwp_ref[...], preferred_element_type=f32)
    xo_ref[...] = x + _rms(gate * emb, g5_ref[...])


def _tail_specs(tm, d, dp):
    row = lambda i: (i, 0)
    const = lambda i: (0, 0)
    vec = pl.BlockSpec((1, d), const)
    once = pl.Buffered(1)
    in_specs = [vec, vec, pl.BlockSpec((tm, dp), row),
                pl.BlockSpec((d, d), const, pipeline_mode=once),
                pl.BlockSpec((dp, d), const, pipeline_mode=once), vec]
    return in_specs, pl.BlockSpec((tm, d), row)


def _swiglu(h, wg_ref, wu_ref, wd_ref, act, tf):
    for c in range(act.shape[1] // tf):
        sl = slice(c * tf, (c + 1) * tf)
        gate = jnp.dot(h, wg_ref[:, sl], preferred_element_type=f32)
        up = jnp.dot(h, wu_ref[:, sl], preferred_element_type=f32)
        act[:, sl] = (gate * _sigmoid(gate) * up).astype(bf16)
    return jnp.dot(act[...], wd_ref[...], preferred_element_type=f32)


def _ffn_kernel(h_ref, x_ref, wg_ref, wu_ref, wd_ref, *rest, tf):
    *tail, act = rest
    _mixer_tail(x_ref[...], _swiglu(h_ref[...], wg_ref, wu_ref, wd_ref, act, tf), *tail)


def ffn(h, x, wg, wu, wd, tail, tm, tf):
    t, d = x.shape
    dff = wg.shape[1]
    row = lambda i: (i, 0)
    const = lambda i: (0, 0)
    once = pl.Buffered(1)
    tail_in, outs = _tail_specs(tm, d, tail[2].shape[1])
    return pl.pallas_call(
        functools.partial(_ffn_kernel, tf=tf),
        grid=(t // tm,),
        in_specs=[pl.BlockSpec((tm, d), row), pl.BlockSpec((tm, d), row),
                  pl.BlockSpec((d, dff), const, pipeline_mode=once),
                  pl.BlockSpec((d, dff), const, pipeline_mode=once),
                  pl.BlockSpec((dff, d), const, pipeline_mode=once)] + tail_in,
        out_specs=outs,
        out_shape=jax.ShapeDtypeStruct((t, d), f32),
        scratch_shapes=[pltpu.VMEM((tm, dff), bf16)],
        compiler_params=_cparams("parallel"),
        name="ffn",
    )(h, x, wg, wu, wd, *tail)


def _router_kernel(h_ref, rw_ref, idx_ref, w_ref, cnt_ref, tri, cnt):
    tm = h_ref.shape[0]

    @pl.when(pl.program_id(0) == 0)
    def _():
        before = lax.broadcasted_iota(jnp.int32, (tm, tm), 0) < lax.broadcasted_iota(jnp.int32, (tm, tm), 1)
        tri[...] = jnp.where(before, 1.0, 0.0).astype(bf16)
        cnt[...] = jnp.zeros_like(cnt)

    logits = lax.dot_general(rw_ref[...], h_ref[...].astype(bf16), (((1,), (1,)), ((), ())),
                             preferred_element_type=f32)
    e_idx = lax.broadcasted_iota(jnp.int32, logits.shape, 0)
    n_e = logits.shape[0]
    v1 = jnp.max(logits, axis=0, keepdims=True)
    i1 = jnp.min(jnp.where(logits == v1, e_idx, n_e), axis=0, keepdims=True)
    rest = jnp.where(e_idx == i1, -jnp.inf, logits)
    v2 = jnp.max(rest, axis=0, keepdims=True)
    i2 = jnp.min(jnp.where(rest == v2, e_idx, n_e), axis=0, keepdims=True)
    ex = jnp.exp(v2 - v1)
    w_ref[0:1, :] = 1.0 / (1.0 + ex)
    w_ref[1:2, :] = ex / (1.0 + ex)

    sel1 = e_idx == i1
    sel2 = e_idx == i2
    sel = jnp.where(jnp.logical_or(sel1, sel2), 1.0, 0.0)
    prefix = jnp.dot(sel.astype(bf16), tri[...], preferred_element_type=f32)
    rank = prefix + cnt[:, 0:1]
    idx_ref[0:1, :] = i1
    idx_ref[1:2, :] = i2
    idx_ref[2:3, :] = jnp.sum(jnp.where(sel1, rank, 0.0), axis=0, keepdims=True).astype(jnp.int32)
    idx_ref[3:4, :] = jnp.sum(jnp.where(sel2, rank, 0.0), axis=0, keepdims=True).astype(jnp.int32)
    cnt[...] = cnt[...] + jnp.sum(sel, axis=1, keepdims=True)
    cnt_ref[...] = cnt[...]


def router(h, rw_t, tm):
    t, d = h.shape
    n_e = rw_t.shape[0]
    return pl.pallas_call(
        _router_kernel,
        grid=(t // tm,),
        in_specs=[pl.BlockSpec((tm, d), lambda i: (i, 0)),
                  pl.BlockSpec((n_e, d), lambda i: (0, 0))],
        out_specs=[pl.BlockSpec((4, tm), lambda i: (0, i)),
                   pl.BlockSpec((2, tm), lambda i: (0, i)),
                   pl.BlockSpec((n_e, LANES), lambda i: (0, 0))],
        out_shape=[jax.ShapeDtypeStruct((4, t), jnp.int32),
                   jax.ShapeDtypeStruct((2, t), f32),
                   jax.ShapeDtypeStruct((n_e, LANES), f32)],
        scratch_shapes=[pltpu.VMEM((tm, tm), bf16), pltpu.VMEM((n_e, LANES), f32)],
        compiler_params=_cparams("arbitrary"),
        name="router",
    )(h, rw_t)


def _dispatch_kernel(slot1_ref, slot2_ref, h_ref, xs_in, xs_hbm, sem):
    del xs_in
    tc = h_ref.shape[0]

    def body(t, carry):
        for slot_ref in (slot1_ref, slot2_ref):
            pltpu.make_async_copy(h_ref.at[pl.ds(t, 1)], xs_hbm.at[pl.ds(slot_ref[t], 1)], sem).start()
        return carry

    lax.fori_loop(0, tc, body, 0, unroll=8)
    for _ in range(TOP_K):
        pltpu.make_async_copy(h_ref, xs_hbm.at[pl.ds(0, tc)], sem).wait()


def dispatch(slots, h, n_slots, tc):
    t, d = h.shape
    smem_1d = pl.BlockSpec((tc,), lambda i: (i,), memory_space=pltpu.SMEM)
    return pl.pallas_call(
        _dispatch_kernel,
        grid=(t // tc,),
        in_specs=[smem_1d, smem_1d,
                  pl.BlockSpec((tc, d), lambda i: (i, 0)),
                  pl.BlockSpec(memory_space=pl.ANY)],
        out_specs=pl.BlockSpec(memory_space=pl.ANY),
        out_shape=jax.ShapeDtypeStruct((n_slots, d), h.dtype),
        scratch_shapes=[pltpu.SemaphoreType.DMA(())],
        input_output_aliases={3: 0},
        compiler_params=_cparams("arbitrary"),
        name="dispatch",
    )(slots[0], slots[1], h, jnp.zeros((n_slots, d), h.dtype))


def _experts_kernel(te_ref, rt_ref, nu_ref, xs_ref, wg_ref, wu_ref, wd_ref, ys_ref, act, *, tf):
    del te_ref, rt_ref

    @pl.when(pl.program_id(0) < nu_ref[0])
    def _():
        ys_ref[...] = _swiglu(xs_ref[...].astype(bf16), wg_ref.at[0], wu_ref.at[0], wd_ref.at[0], act, tf)

    @pl.when(pl.program_id(0) >= nu_ref[0])
    def _():
        ys_ref[...] = jnp.zeros_like(ys_ref)


def experts(tile_expert, row_tile, n_used, xs, wg, wu, wd, te, tf):
    n_slots, d = xs.shape
    dff = wg.shape[2]
    once = pl.Buffered(1)
    return pl.pallas_call(
        functools.partial(_experts_kernel, tf=tf),
        grid_spec=pltpu.PrefetchScalarGridSpec(
            num_scalar_prefetch=3,
            grid=(n_slots // te,),
            in_specs=[pl.BlockSpec((te, d), lambda i, et, rt, nu: (rt[i], 0)),
                      pl.BlockSpec((1, d, dff), lambda i, et, rt, nu: (et[i], 0, 0), pipeline_mode=once),
                      pl.BlockSpec((1, d, dff), lambda i, et, rt, nu: (et[i], 0, 0), pipeline_mode=once),
                      pl.BlockSpec((1, dff, d), lambda i, et, rt, nu: (et[i], 0, 0), pipeline_mode=once)],
            out_specs=pl.BlockSpec((te, d), lambda i, et, rt, nu: (i, 0)),
            scratch_shapes=[pltpu.VMEM((te, dff), bf16)]),
        out_shape=jax.ShapeDtypeStruct((n_slots, d), f32),
        compiler_params=_cparams("arbitrary"),
        name="experts",
    )(tile_expert, row_tile, n_used, xs, wg, wu, wd)


def _combine_kernel(slot1_ref, slot2_ref, ys_hbm, w_ref, x_ref, *rest):
    *tail, buf, sem = rest
    tm = x_ref.shape[0]

    def body(t, carry):
        for k, slot_ref in enumerate((slot1_ref, slot2_ref)):
            pltpu.make_async_copy(ys_hbm.at[pl.ds(slot_ref[t], 1)], buf.at[k, pl.ds(t, 1)], sem).start()
        return carry

    lax.fori_loop(0, tm, body, 0, unroll=8)
    for k in range(TOP_K):
        pltpu.make_async_copy(ys_hbm.at[pl.ds(0, tm)], buf.at[k], sem).wait()

    w = w_ref[...]
    _mixer_tail(x_ref[...], w[:, 0:1] * buf[0] + w[:, 1:2] * buf[1], *tail)


def combine(slots, ys, w_cols, x, tail, tm):
    t, d = x.shape
    row = lambda i: (i, 0)
    smem_1d = pl.BlockSpec((tm,), lambda i: (i,), memory_space=pltpu.SMEM)
    tail_in, outs = _tail_specs(tm, d, tail[2].shape[1])
    return pl.pallas_call(
        _combine_kernel,
        grid=(t // tm,),
        in_specs=[smem_1d, smem_1d,
                  pl.BlockSpec(memory_space=pl.ANY),
                  pl.BlockSpec((tm, TOP_K), row), pl.BlockSpec((tm, d), row)] + tail_in,
        out_specs=outs,
        out_shape=jax.ShapeDtypeStruct((t, d), f32),
        scratch_shapes=[pltpu.VMEM((TOP_K, tm, d), f32), pltpu.SemaphoreType.DMA(())],
        compiler_params=_cparams("arbitrary"),
        name="combine",
    )(slots[0], slots[1], ys, w_cols, x, *tail)


def moe_routed(h, x, rw_t, wg, wu, wd, tail, tm, te, tf):
    t, d = h.shape
    n_e = wg.shape[0]
    idx, w_rows, cnt = router(h, rw_t, tm)
    counts = cnt[:, 0].astype(jnp.int32)
    padded = (counts + te - 1) // te * te
    ends = jnp.cumsum(padded)
    offs = ends - padded
    slots = jnp.stack([offs[idx[0]] + idx[2], offs[idx[1]] + idx[3]])
    n_tiles = TOP_K * t // te + n_e
    n_used = ends[-1] // te
    tiles = jnp.minimum(jnp.arange(n_tiles, dtype=jnp.int32), n_used - 1)
    tile_expert = jnp.sum((tiles[:, None] >= (ends // te)[None, :]).astype(jnp.int32), axis=1)
    xs = dispatch(slots, h, n_tiles * te, tm)
    ys = experts(tile_expert, tiles, n_used.reshape(1).astype(jnp.int32), xs, wg, wu, wd, te, tf)
    return combine(slots, ys, w_rows.T, x, tail, min(tm, 512))


def _gate_weights(wa, wx):
    per = MXU_DIM // wa.shape[1]

    def tiles(w):
        groups = w.reshape(w.shape[0] // per, per, w.shape[1], w.shape[2])
        eye = jnp.eye(per, dtype=w.dtype)
        return jnp.einsum('gpcd,pq->gpcqd', groups, eye).reshape(groups.shape[0], MXU_DIM, MXU_DIM)

    return jnp.concatenate([tiles(wa), tiles(wx)], axis=-1).astype(bf16)


def kernel(x, p, positions, norm_g, w_in, conv_w, conv_b, lru_wa, lru_ba, lru_wx, lru_bx, lru_lam,
           diff_lambda, subln_g, w_proj_a, w_proj_b, w_out, ffn_w_gate, ffn_w_up, ffn_w_down, router_w,
           moe_w_gate, moe_w_up, moe_w_down, ple_w_proj, ple_w_gate):
    batch, seq, d = x.shape
    depth = w_in.shape[0]
    t = batch * seq
    tm = min(1024, t)
    tq = min(512, seq)
    te = 512
    tf = 512 if ffn_w_gate.shape[-1] % 512 == 0 else ffn_w_gate.shape[-1]

    xf = x.reshape(t, d)
    pos_f = positions.reshape(t, 1).astype(f32)
    inv_freq = ROPE_THETA ** (-jnp.arange(0, HEAD_DIM, 2, dtype=f32) / HEAD_DIM)
    inv_row = jnp.tile(inv_freq, LANES // (HEAD_DIM // 2)).reshape(1, LANES)
    cos, sin = rope_tables(pos_f, inv_row, tm)
    row = lambda v: v.reshape(1, -1)

    for i in range(depth):
        lambda_init = 0.8 - 0.6 * math.exp(-0.3 * i)
        proj = in_proj(xf, row(norm_g[i, 0]), w_in[i].astype(bf16), cos, sin, min(tm, 512), d)
        ya = rglru(proj, conv_w[i], row(conv_b[i]), _gate_weights(lru_wa[i], lru_wx[i]),
                   row(lru_ba[i]), row(lru_bx[i]), row(lru_lam[i]), batch, tq)
        yb = diff_attn(proj, diff_lambda[i], row(subln_g[i]), batch, tq, lambda_init)
        dense = i % 2 == 0
        xf, h = merge(ya, yb, proj, xf, w_proj_a[i].astype(bf16), w_proj_b[i].astype(bf16),
                      w_out[i].astype(bf16), row(norm_g[i, 1]), row(norm_g[i, 2]), tm, bf16 if dense else f32)
        tail = (row(norm_g[i, 3]), row(norm_g[i, 4]), p[i].reshape(t, -1), ple_w_gate[i].astype(bf16),
                ple_w_proj[i].astype(bf16), row(norm_g[i, 5]))
        if dense:
            xf = ffn(h, xf, ffn_w_gate[i // 2].astype(bf16), ffn_w_up[i // 2].astype(bf16),
                        ffn_w_down[i // 2].astype(bf16), tail, min(tm, 512), tf)
        else:
            xf = moe_routed(h, xf, router_w[i // 2].T.astype(bf16), moe_w_gate[i // 2].astype(bf16),
                        moe_w_up[i // 2].astype(bf16), moe_w_down[i // 2].astype(bf16), tail, tm, te, tf)
    return xf.reshape(batch, seq, d)
```

```python
import functools
import math

import jax
import jax.numpy as jnp
from jax import lax
from jax.experimental import pallas as pl
from jax.experimental.pallas import tpu as pltpu

EPS = 1e-6
CONV_WIDTH = 4
LRU_C = 8.0
N_RNN_BLOCKS = 16
N_HEADS = 8
HEAD_DIM = 64
V_DIM = 2 * HEAD_DIM
ROPE_THETA = 10000.0
TOP_K = 2
LANES = 128
SUBLANES = 8
MXU_DIM = 256
VMEM_LIMIT = 56 * 1024 * 1024
NEG_BIG = -1e30
ATTN_SCALE = HEAD_DIM ** -0.5 * math.log2(math.e)

f32 = jnp.float32
bf16 = jnp.bfloat16


def _cparams(*sem):
    return pltpu.CompilerParams(dimension_semantics=sem, vmem_limit_bytes=VMEM_LIMIT)


def _rms(x, g):
    return x * lax.rsqrt(jnp.mean(x * x, axis=-1, keepdims=True) + EPS) * g


def _sigmoid(x):
    return 1.0 / (1.0 + jnp.exp(-x))


def _sigmoid_tanh(x):
    return 0.5 * (1.0 + jnp.tanh(0.5 * x))


def _rope_tab_kernel(pos_ref, inv_ref, cos_ref, sin_ref):
    ang = pos_ref[...] * inv_ref[...]
    cos_ref[...] = jnp.cos(ang)
    sin_ref[...] = jnp.sin(ang)


def rope_tables(pos_f, inv_row, tm):
    t = pos_f.shape[0]
    return pl.pallas_call(
        _rope_tab_kernel,
        grid=(t // tm,),
        in_specs=[pl.BlockSpec((tm, 1), lambda i: (i, 0)),
                  pl.BlockSpec((1, LANES), lambda i: (0, 0))],
        out_specs=[pl.BlockSpec((tm, LANES), lambda i: (i, 0))] * 2,
        out_shape=[jax.ShapeDtypeStruct((t, LANES), f32)] * 2,
        compiler_params=_cparams("parallel"),
        name="rope_tables",
    )(pos_f, inv_row)


def _in_proj_kernel(x_ref, g_ref, w_ref, cos_ref, sin_ref, o_ref, *, tn, q_tile, k_tile):
    h = _rms(x_ref[...], g_ref[...]).astype(bf16)
    for j in range(w_ref.shape[1] // tn):
        cols = slice(j * tn, (j + 1) * tn)
        acc = jnp.dot(h, w_ref[:, cols], preferred_element_type=f32)
        if j not in (q_tile, k_tile):
            o_ref[:, cols] = acc.astype(o_ref.dtype)
            continue
        scale = ATTN_SCALE if j == q_tile else 1.0
        cos = cos_ref[...] * scale
        sin = sin_ref[...] * scale
        lane = lax.broadcasted_iota(jnp.int32, cos.shape, 1)
        first_half = (lane % HEAD_DIM) < (HEAD_DIM // 2)
        for g in range(tn // LANES):
            t = acc[:, g * LANES:(g + 1) * LANES]
            up = pltpu.roll(t, HEAD_DIM // 2, axis=1)
            down = pltpu.roll(t, LANES - HEAD_DIM // 2, axis=1)
            rot = jnp.where(first_half, -down, up)
            lanes = slice(j * tn + g * LANES, j * tn + (g + 1) * LANES)
            o_ref[:, lanes] = (t * cos + rot * sin).astype(o_ref.dtype)


def in_proj(x, g, w, cos, sin, tm, tn):
    t, d = x.shape
    n = w.shape[1]
    kern = functools.partial(_in_proj_kernel, tn=tn, q_tile=2 * d // tn, k_tile=3 * d // tn)
    return pl.pallas_call(
        kern,
        grid=(t // tm,),
        in_specs=[pl.BlockSpec((tm, d), lambda i: (i, 0)),
                  pl.BlockSpec((1, d), lambda i: (0, 0)),
                  pl.BlockSpec((d, n), lambda i: (0, 0), pipeline_mode=pl.Buffered(1)),
                  pl.BlockSpec((tm, LANES), lambda i: (i, 0)),
                  pl.BlockSpec((tm, LANES), lambda i: (i, 0))],
        out_specs=pl.BlockSpec((tm, n), lambda i: (i, 0)),
        out_shape=jax.ShapeDtypeStruct((t, n), bf16),
        compiler_params=_cparams("parallel"),
        name="in_proj",
    )(x, g, w, cos, sin)


def _rglru_kernel(gate_ref, x_ref, cw_ref, cb_ref, wbd_ref, ba_ref, bx_ref, lam_ref, o_ref,
                  xbuf, a_s, b_s, hcar, *, tc):
    c = pl.program_id(1)
    d = x_ref.shape[1]

    @pl.when(c == 0)
    def _():
        xbuf[0:SUBLANES, :] = jnp.zeros((SUBLANES, d), f32)
        hcar[...] = jnp.zeros_like(hcar)

    @pl.when(c > 0)
    def _():
        xbuf[0:SUBLANES, :] = xbuf[tc:tc + SUBLANES, :]

    xbuf[SUBLANES:SUBLANES + tc, :] = x_ref[...].astype(f32)

    u = cb_ref[...] + cw_ref[CONV_WIDTH - 1:CONV_WIDTH, :] * xbuf[SUBLANES:SUBLANES + tc, :]
    for k in range(CONV_WIDTH - 1):
        off = SUBLANES - (CONV_WIDTH - 1) + k
        u = u + cw_ref[k:k + 1, :] * xbuf[off:off + tc, :]

    ub = u.astype(bf16)
    z = -lam_ref[...]
    softplus = jnp.maximum(z, 0.0) + jnp.log(1.0 + jnp.exp(-jnp.abs(z)))
    for g in range(d // MXU_DIM):
        sl = slice(g * MXU_DIM, (g + 1) * MXU_DIM)
        zz = jnp.dot(ub[:, sl], wbd_ref[g], preferred_element_type=f32)
        r = _sigmoid_tanh(zz[:, :MXU_DIM] + ba_ref[:, sl])
        i_gate = _sigmoid_tanh(zz[:, MXU_DIM:] + bx_ref[:, sl])
        log_a = (-LRU_C) * r * softplus[:, sl]
        a = jnp.exp(log_a)
        beta = jnp.sqrt(1.0 - a * a)
        a_s[:, sl] = a
        b_s[:, sl] = beta * (i_gate * u[:, sl])

    row = lax.broadcasted_iota(jnp.int32, (SUBLANES, d), 0)

    def body(i, h):
        rows = pl.ds(pl.multiple_of(i * SUBLANES, SUBLANES), SUBLANES)
        a = a_s[rows, :]
        b = b_s[rows, :]
        shift = 1
        while shift < SUBLANES:
            keep = row >= shift
            b = a * jnp.where(keep, pltpu.roll(b, shift, axis=0), 0.0) + b
            a = a * jnp.where(keep, pltpu.roll(a, shift, axis=0), 1.0)
            shift *= 2
        h_tile = a * h + b
        b_s[rows, :] = h_tile
        return h_tile[SUBLANES - 1:SUBLANES, :]

    hcar[...] = lax.fori_loop(0, tc // SUBLANES, body, hcar[...], unroll=2)

    gt = gate_ref[...].astype(f32)
    gelu = 0.5 * gt * (1.0 + jnp.tanh(math.sqrt(2.0 / math.pi) * (gt + 0.044715 * (gt * gt * gt))))
    o_ref[...] = (b_s[...] * gelu).astype(o_ref.dtype)


def rglru(proj, cw, cb, wbd, ba, bx, lam, batch, tc):
    t = proj.shape[0]
    d = cw.shape[1]
    nc = t // batch // tc
    row = lambda b, c: (b * nc + c, 0)
    const2 = lambda b, c: (0, 0)
    return pl.pallas_call(
        functools.partial(_rglru_kernel, tc=tc),
        grid=(batch, nc),
        in_specs=[pl.BlockSpec((tc, d), lambda b, c: (b * nc + c, 0)),
                  pl.BlockSpec((tc, d), lambda b, c: (b * nc + c, 1)),
                  pl.BlockSpec((CONV_WIDTH, d), const2),
                  pl.BlockSpec((1, d), const2),
                  pl.BlockSpec(wbd.shape, lambda b, c: (0, 0, 0)),
                  pl.BlockSpec((1, d), const2),
                  pl.BlockSpec((1, d), const2),
                  pl.BlockSpec((1, d), const2)],
        out_specs=pl.BlockSpec((tc, d), row),
        out_shape=jax.ShapeDtypeStruct((t, d), bf16),
        scratch_shapes=[pltpu.VMEM((tc + SUBLANES, d), f32),
                        pltpu.VMEM((tc, d), f32),
                        pltpu.VMEM((tc, d), f32),
                        pltpu.VMEM((1, d), f32)],
        compiler_params=_cparams("parallel", "arbitrary"),
        name="rglru",
    )(proj, proj, cw, cb, wbd, ba, bx, lam)


def _diff_attn_kernel(q_ref, k_ref, v_ref, dl_ref, gs_ref, o_ref, vt, qt, s_a, s_b, m_s, l_s, acc,
                      *, tq, lambda_init):
    n_kv = k_ref.shape[0] // tq
    for c in range(n_kv):
        vt[c] = v_ref[c * tq:(c + 1) * tq, :].astype(f32).T.astype(bf16)
    dl = dl_ref[...]
    lam = (jnp.exp(jnp.sum(dl[0:1, :] * dl[1:2, :], axis=-1, keepdims=True))
           - jnp.exp(jnp.sum(dl[2:3, :] * dl[3:4, :], axis=-1, keepdims=True)) + lambda_init)

    def rows_of(tile):
        return slice(tile * tq, (tile + 1) * tq)

    def scores(item, s_ref):
        w, j, _ = item
        s_ref[...] = jnp.dot(k_ref[rows_of(j), :], qt[w], preferred_element_type=f32)

    def consume(item, s_ref):
        w, j, masked = item
        s_t = s_ref[...]
        if masked:
            key = lax.broadcasted_iota(jnp.int32, s_t.shape, 0)
            query = lax.broadcasted_iota(jnp.int32, s_t.shape, 1) % tq
            s_t = jnp.where(query >= key, s_t, NEG_BIG)
        m_old = m_s[w]
        m_new = jnp.maximum(m_old, jnp.max(s_t, axis=0, keepdims=True))
        alpha = jnp.exp2(m_old - m_new)
        p_t = jnp.exp2(s_t - m_new)
        l_s[w] = alpha * l_s[w] + jnp.sum(p_t, axis=0, keepdims=True)
        acc[w] = alpha * acc[w] + jnp.dot(vt[j], p_t.astype(bf16), preferred_element_type=f32)
        m_s[w] = m_new

    def finalize(w, q_tile):
        o_all = acc[w] * (1.0 / l_s[w])
        o_t = o_all[:, 0:tq] - lam * o_all[:, tq:2 * tq]
        o_t = o_t * lax.rsqrt(jnp.mean(o_t * o_t, axis=0, keepdims=True) + EPS)
        o_ref[rows_of(q_tile), :] = (o_t.T * gs_ref[...] * (1.0 - lambda_init)).astype(o_ref.dtype)

    items = [(q, j, j == q) for q in range(n_kv) for j in range(q + 1)]
    bufs = (s_a, s_b)

    def start_query_tile(q):
        w = q % 2
        q_t = q_ref[rows_of(q), :].astype(f32).T
        row = lax.broadcasted_iota(jnp.int32, q_t.shape, 0)
        qt[w, :, 0:tq] = jnp.where(row < HEAD_DIM, q_t, 0.0).astype(bf16)
        qt[w, :, tq:2 * tq] = jnp.where(row >= HEAD_DIM, q_t, 0.0).astype(bf16)
        m_s[w] = jnp.full(m_s.shape[1:], NEG_BIG, f32)
        l_s[w] = jnp.zeros(l_s.shape[1:], f32)
        acc[w] = jnp.zeros(acc.shape[1:], f32)

    start_query_tile(0)
    scores((0, 0, True), bufs[0])
    for i, (q, j, masked) in enumerate(items):
        if i + 1 < len(items):
            qn, jn, mn = items[i + 1]
            if jn == 0:
                start_query_tile(qn)
            scores((qn % 2, jn, mn), bufs[(i + 1) % 2])
        consume((q % 2, j, masked), bufs[i % 2])
        if masked:
            finalize(q % 2, q)


def diff_attn(proj, dl, gs, batch, tq, lambda_init):
    t = proj.shape[0]
    seq = t // batch
    nq = seq // tq
    d = N_HEADS * V_DIM
    q_col, k_col, v_col = 2 * N_HEADS, 3 * N_HEADS, 4 * N_HEADS
    kern = functools.partial(_diff_attn_kernel, tq=tq, lambda_init=lambda_init)
    return pl.pallas_call(
        kern,
        grid=(batch, N_HEADS),
        in_specs=[pl.BlockSpec((seq, V_DIM), lambda b, h: (b, q_col + h)),
                  pl.BlockSpec((seq, V_DIM), lambda b, h: (b, k_col + h)),
                  pl.BlockSpec((seq, V_DIM), lambda b, h: (b, v_col + h)),
                  pl.BlockSpec((4, HEAD_DIM), lambda b, h: (0, 0)),
                  pl.BlockSpec((1, V_DIM), lambda b, h: (0, 0))],
        out_specs=pl.BlockSpec((seq, V_DIM), lambda b, h: (b, h)),
        out_shape=jax.ShapeDtypeStruct((t, d), bf16),
        scratch_shapes=[pltpu.VMEM((nq, V_DIM, tq), bf16),
                        pltpu.VMEM((2, V_DIM, 2 * tq), bf16),
                        pltpu.VMEM((tq, 2 * tq), f32),
                        pltpu.VMEM((tq, 2 * tq), f32),
                        pltpu.VMEM((2, 1, 2 * tq), f32),
                        pltpu.VMEM((2, 1, 2 * tq), f32),
                        pltpu.VMEM((2, V_DIM, 2 * tq), f32)],
        compiler_params=_cparams("parallel", "parallel"),
        name="diff_attn",
    )(proj, proj, proj, dl, gs)


def _merge_kernel(ya_ref, yb_ref, ga_ref, gb_ref, x_ref, wa_ref, wb_ref, wo_ref, g1_ref, g2_ref,
                  xo_ref, ho_ref):
    pa = jnp.dot(ya_ref[...], wa_ref[...], preferred_element_type=f32)
    pb = jnp.dot(yb_ref[...], wb_ref[...], preferred_element_type=f32)
    merged = _sigmoid(ga_ref[...].astype(f32)) * pa + _sigmoid(gb_ref[...].astype(f32)) * pb
    o = jnp.dot(merged.astype(bf16), wo_ref[...], preferred_element_type=f32)
    x_new = x_ref[...] + _rms(o, g1_ref[...])
    xo_ref[...] = x_new
    ho_ref[...] = _rms(x_new, g2_ref[...]).astype(ho_ref.dtype)


def merge(ya, yb, proj, x, wa, wb, wo, g1, g2, tm, h_dtype):
    t, d = x.shape
    ga_col = proj.shape[1] // d - 2
    row = lambda i: (i, 0)
    const = lambda i: (0, 0)
    return pl.pallas_call(
        _merge_kernel,
        grid=(t // tm,),
        in_specs=[pl.BlockSpec((tm, d), row), pl.BlockSpec((tm, d), row),
                  pl.BlockSpec((tm, d), lambda i: (i, ga_col)),
                  pl.BlockSpec((tm, d), lambda i: (i, ga_col + 1)),
                  pl.BlockSpec((tm, d), row),
                  pl.BlockSpec((d, d), const), pl.BlockSpec((d, d), const), pl.BlockSpec((d, d), const),
                  pl.BlockSpec((1, d), const), pl.BlockSpec((1, d), const)],
        out_specs=[pl.BlockSpec((tm, d), row), pl.BlockSpec((tm, d), row)],
        out_shape=[jax.ShapeDtypeStruct((t, d), f32), jax.ShapeDtypeStruct((t, d), h_dtype)],
        compiler_params=_cparams("parallel"),
        name="merge",
    )(ya, yb, proj, proj, x, wa, wb, wo, g1, g2)


def _mixer_tail(x, f, g3_ref, g4_ref, p_ref, pwg_ref, pwp_ref, g5_ref, xo_ref):
    x = x + _rms(f, g3_ref[...])
    gate = _sigmoid(jnp.dot(_rms(x, g4_ref[...]).astype(bf16), pwg_ref[...], preferred_element_type=f32))
    emb = jnp.dot(p_ref[...].astype(bf16), pwp_ref[...], preferred_element_type=f32)
    xo_ref[...] = x + _rms(gate * emb, g5_ref[...])


def _tail_specs(tm, d, dp):
    row = lambda i: (i, 0)
    const = lambda i: (0, 0)
    vec = pl.BlockSpec((1, d), const)
    once = pl.Buffered(1)
    in_specs = [vec, vec, pl.BlockSpec((tm, dp), row),
                pl.BlockSpec((d, d), const, pipeline_mode=once),
                pl.BlockSpec((dp, d), const, pipeline_mode=once), vec]
    return in_specs, pl.BlockSpec((tm, d), row)


def _swiglu(h, wg_ref, wu_ref, wd_ref, act, tf):
    for c in range(act.shape[1] // tf):
        sl = slice(c * tf, (c + 1) * tf)
        gate = jnp.dot(h, wg_ref[:, sl], preferred_element_type=f32)
        up = jnp.dot(h, wu_ref[:, sl], preferred_element_type=f32)
        act[:, sl] = (gate * _sigmoid(gate) * up).astype(bf16)
    return jnp.dot(act[...], wd_ref[...], preferred_element_type=f32)


def _ffn_kernel(h_ref, x_ref, wg_ref, wu_ref, wd_ref, *rest, tf):
    *tail, act = rest
    _mixer_tail(x_ref[...], _swiglu(h_ref[...], wg_ref, wu_ref, wd_ref, act, tf), *tail)


def ffn(h, x, wg, wu, wd, tail, tm, tf):
    t, d = x.shape
    dff = wg.shape[1]
    row = lambda i: (i, 0)
    const = lambda i: (0, 0)
    once = pl.Buffered(1)
    tail_in, outs = _tail_specs(tm, d, tail[2].shape[1])
    return pl.pallas_call(
        functools.partial(_ffn_kernel, tf=tf),
        grid=(t // tm,),
        in_specs=[pl.BlockSpec((tm, d), row), pl.BlockSpec((tm, d), row),
                  pl.BlockSpec((d, dff), const, pipeline_mode=once),
                  pl.BlockSpec((d, dff), const, pipeline_mode=once),
                  pl.BlockSpec((dff, d), const, pipeline_mode=once)] + tail_in,
        out_specs=outs,
        out_shape=jax.ShapeDtypeStruct((t, d), f32),
        scratch_shapes=[pltpu.VMEM((tm, dff), bf16)],
        compiler_params=_cparams("parallel"),
        name="ffn",
    )(h, x, wg, wu, wd, *tail)


def _router_kernel(h_ref, rw_ref, idx_ref, w_ref, cnt_ref, tri, cnt):
    tm = h_ref.shape[0]

    @pl.when(pl.program_id(0) == 0)
    def _():
        before = lax.broadcasted_iota(jnp.int32, (tm, tm), 0) < lax.broadcasted_iota(jnp.int32, (tm, tm), 1)
        tri[...] = jnp.where(before, 1.0, 0.0).astype(bf16)
        cnt[...] = jnp.zeros_like(cnt)

    logits = lax.dot_general(rw_ref[...], h_ref[...].astype(bf16), (((1,), (1,)), ((), ())),
                             preferred_element_type=f32)
    e_idx = lax.broadcasted_iota(jnp.int32, logits.shape, 0)
    n_e = logits.shape[0]
    v1 = jnp.max(logits, axis=0, keepdims=True)
    i1 = jnp.min(jnp.where(logits == v1, e_idx, n_e), axis=0, keepdims=True)
    rest = jnp.where(e_idx == i1, -jnp.inf, logits)
    v2 = jnp.max(rest, axis=0, keepdims=True)
    i2 = jnp.min(jnp.where(rest == v2, e_idx, n_e), axis=0, keepdims=True)
    ex = jnp.exp(v2 - v1)
    w_ref[0:1, :] = 1.0 / (1.0 + ex)
    w_ref[1:2, :] = ex / (1.0 + ex)

    sel1 = e_idx == i1
    sel2 = e_idx == i2
    sel = jnp.where(jnp.logical_or(sel1, sel2), 1.0, 0.0)
    prefix = jnp.dot(sel.astype(bf16), tri[...], preferred_element_type=f32)
    rank = prefix + cnt[:, 0:1]
    idx_ref[0:1, :] = i1
    idx_ref[1:2, :] = i2
    idx_ref[2:3, :] = jnp.sum(jnp.where(sel1, rank, 0.0), axis=0, keepdims=True).astype(jnp.int32)
    idx_ref[3:4, :] = jnp.sum(jnp.where(sel2, rank, 0.0), axis=0, keepdims=True).astype(jnp.int32)
    cnt[...] = cnt[...] + jnp.sum(sel, axis=1, keepdims=True)
    cnt_ref[...] = cnt[...]


def router(h, rw_t, tm):
    t, d = h.shape
    n_e = rw_t.shape[0]
    return pl.pallas_call(
        _router_kernel,
        grid=(t // tm,),
        in_specs=[pl.BlockSpec((tm, d), lambda i: (i, 0)),
                  pl.BlockSpec((n_e, d), lambda i: (0, 0))],
        out_specs=[pl.BlockSpec((4, tm), lambda i: (0, i)),
                   pl.BlockSpec((2, tm), lambda i: (0, i)),
                   pl.BlockSpec((n_e, LANES), lambda i: (0, 0))],
        out_shape=[jax.ShapeDtypeStruct((4, t), jnp.int32),
                   jax.ShapeDtypeStruct((2, t), f32),
                   jax.ShapeDtypeStruct((n_e, LANES), f32)],
        scratch_shapes=[pltpu.VMEM((tm, tm), bf16), pltpu.VMEM((n_e, LANES), f32)],
        compiler_params=_cparams("arbitrary"),
        name="router",
    )(h, rw_t)


def _dispatch_kernel(slot1_ref, slot2_ref, ends_ref, h_ref, xs_hbm, zeros, sem, *, te):
    tc = h_ref.shape[0]

    @pl.when(pl.program_id(0) == 0)
    def _():
        zeros[...] = jnp.zeros_like(zeros)
        def clear(start):
            copy = pltpu.make_async_copy(zeros, xs_hbm.at[pl.ds(pl.multiple_of(start, te), te)], sem)
            copy.start()
            copy.wait()

        n_e = ends_ref.shape[0]
        for e in range(n_e):
            clear(jnp.maximum(ends_ref[e] - te, 0))
        for k in range(n_e):
            start = ends_ref[n_e - 1] + k * te
            pl.when(start < xs_hbm.shape[0])(functools.partial(clear, start))

    def body(t, carry):
        for slot_ref in (slot1_ref, slot2_ref):
            pltpu.make_async_copy(h_ref.at[pl.ds(t, 1)], xs_hbm.at[pl.ds(slot_ref[t], 1)], sem).start()
        return carry

    lax.fori_loop(0, tc, body, 0, unroll=8)
    for _ in range(TOP_K):
        pltpu.make_async_copy(h_ref, xs_hbm.at[pl.ds(0, tc)], sem).wait()


def dispatch(slots, ends, h, n_slots, tc, te):
    t, d = h.shape
    smem_1d = pl.BlockSpec((tc,), lambda i: (i,), memory_space=pltpu.SMEM)
    return pl.pallas_call(
        functools.partial(_dispatch_kernel, te=te),
        grid=(t // tc,),
        in_specs=[smem_1d, smem_1d,
                  pl.BlockSpec(ends.shape, lambda i: (0,), memory_space=pltpu.SMEM),
                  pl.BlockSpec((tc, d), lambda i: (i, 0))],
        out_specs=pl.BlockSpec(memory_space=pl.ANY),
        out_shape=jax.ShapeDtypeStruct((n_slots, d), h.dtype),
        scratch_shapes=[pltpu.VMEM((te, d), h.dtype), pltpu.SemaphoreType.DMA(())],
        compiler_params=_cparams("arbitrary"),
        name="dispatch",
    )(slots[0], slots[1], ends, h)


def _experts_kernel(te_ref, rt_ref, nu_ref, xs_ref, wg_ref, wu_ref, wd_ref, ys_ref, act, *, tf):
    del te_ref, rt_ref

    @pl.when(pl.program_id(0) < nu_ref[0])
    def _():
        ys_ref[...] = _swiglu(xs_ref[...].astype(bf16), wg_ref.at[0], wu_ref.at[0], wd_ref.at[0], act, tf)

    @pl.when(pl.program_id(0) >= nu_ref[0])
    def _():
        ys_ref[...] = jnp.zeros_like(ys_ref)


def experts(tile_expert, row_tile, n_used, xs, wg, wu, wd, te, tf):
    n_slots, d = xs.shape
    dff = wg.shape[2]
    once = pl.Buffered(1)
    return pl.pallas_call(
        functools.partial(_experts_kernel, tf=tf),
        grid_spec=pltpu.PrefetchScalarGridSpec(
            num_scalar_prefetch=3,
            grid=(n_slots // te,),
            in_specs=[pl.BlockSpec((te, d), lambda i, et, rt, nu: (rt[i], 0)),
                      pl.BlockSpec((1, d, dff), lambda i, et, rt, nu: (et[i], 0, 0), pipeline_mode=once),
                      pl.BlockSpec((1, d, dff), lambda i, et, rt, nu: (et[i], 0, 0), pipeline_mode=once),
                      pl.BlockSpec((1, dff, d), lambda i, et, rt, nu: (et[i], 0, 0), pipeline_mode=once)],
            out_specs=pl.BlockSpec((te, d), lambda i, et, rt, nu: (i, 0)),
            scratch_shapes=[pltpu.VMEM((te, dff), bf16)]),
        out_shape=jax.ShapeDtypeStruct((n_slots, d), f32),
        compiler_params=_cparams("arbitrary"),
        name="experts",
    )(tile_expert, row_tile, n_used, xs, wg, wu, wd)


def _combine_kernel(slot1_ref, slot2_ref, next1_ref, next2_ref, ys_hbm, w_ref, x_ref, *rest):
    *tail, buf, sem = rest
    tm = x_ref.shape[0]
    i = pl.program_id(0)
    cur = i % 2
    nxt = 1 - cur

    def row_copy(slot_refs, b, t):
        return [pltpu.make_async_copy(ys_hbm.at[pl.ds(slot_ref[t], 1)], buf.at[b, k, pl.ds(t, 1)], sem.at[b])
                for k, slot_ref in enumerate(slot_refs)]

    def wait_rows(b):
        for k in range(TOP_K):
            pltpu.make_async_copy(ys_hbm.at[pl.ds(0, tm)], buf.at[b, k], sem.at[b]).wait()

    @pl.when(i == 0)
    def _():
        def body(t, carry):
            for copy in row_copy((slot1_ref, slot2_ref), 0, t):
                copy.start()
            return carry

        lax.fori_loop(0, tm, body, 0, unroll=8)

    wait_rows(cur)
    for t in range(tm):
        for copy in row_copy((next1_ref, next2_ref), nxt, t):
            copy.start()

    w = w_ref[...]
    _mixer_tail(x_ref[...], w[:, 0:1] * buf[cur, 0] + w[:, 1:2] * buf[cur, 1], *tail)

    @pl.when(i == pl.num_programs(0) - 1)
    def _():
        wait_rows(nxt)


def combine(slots, ys, w_cols, x, tail, tm):
    t, d = x.shape
    n = t // tm
    row = lambda i: (i, 0)
    smem_cur = pl.BlockSpec((tm,), lambda i: (i,), memory_space=pltpu.SMEM)
    smem_next = pl.BlockSpec((tm,), lambda i: (jnp.minimum(i + 1, n - 1),), memory_space=pltpu.SMEM)
    tail_in, outs = _tail_specs(tm, d, tail[2].shape[1])
    return pl.pallas_call(
        _combine_kernel,
        grid=(n,),
        in_specs=[smem_cur, smem_cur, smem_next, smem_next,
                  pl.BlockSpec(memory_space=pl.ANY),
                  pl.BlockSpec((tm, TOP_K), row), pl.BlockSpec((tm, d), row)] + tail_in,
        out_specs=outs,
        out_shape=jax.ShapeDtypeStruct((t, d), f32),
        scratch_shapes=[pltpu.VMEM((2, TOP_K, tm, d), f32), pltpu.SemaphoreType.DMA((2,))],
        compiler_params=_cparams("arbitrary"),
        name="combine",
    )(slots[0], slots[1], slots[0], slots[1], ys, w_cols, x, *tail)


def moe_routed(h, x, rw_t, wg, wu, wd, tail, tm, te, tf):
    t, d = h.shape
    n_e = wg.shape[0]
    idx, w_rows, cnt = router(h, rw_t, tm)
    counts = cnt[:, 0].astype(jnp.int32)
    padded = (counts + te - 1) // te * te
    ends = jnp.cumsum(padded)
    offs = ends - padded
    slots = jnp.stack([offs[idx[0]] + idx[2], offs[idx[1]] + idx[3]])
    n_tiles = TOP_K * t // te + n_e
    n_used = ends[-1] // te
    tiles = jnp.minimum(jnp.arange(n_tiles, dtype=jnp.int32), n_used - 1)
    tile_expert = jnp.sum((tiles[:, None] >= (ends // te)[None, :]).astype(jnp.int32), axis=1)
    xs = dispatch(slots, ends.astype(jnp.int32), h, n_tiles * te, tm, te)
    ys = experts(tile_expert, tiles, n_used.reshape(1).astype(jnp.int32), xs, wg, wu, wd, te, tf)
    return combine(slots, ys, w_rows.T, x, tail, min(tm, 512))


def _gate_weights(wa, wx):
    per = MXU_DIM // wa.shape[1]

    def tiles(w):
        groups = w.reshape(w.shape[0] // per, per, w.shape[1], w.shape[2])
        eye = jnp.eye(per, dtype=w.dtype)
        return jnp.einsum('gpcd,pq->gpcqd', groups, eye).reshape(groups.shape[0], MXU_DIM, MXU_DIM)

    return jnp.concatenate([tiles(wa), tiles(wx)], axis=-1).astype(bf16)


def kernel(x, p, positions, norm_g, w_in, conv_w, conv_b, lru_wa, lru_ba, lru_wx, lru_bx, lru_lam,
           diff_lambda, subln_g, w_proj_a, w_proj_b, w_out, ffn_w_gate, ffn_w_up, ffn_w_down, router_w,
           moe_w_gate, moe_w_up, moe_w_down, ple_w_proj, ple_w_gate):
    batch, seq, d = x.shape
    depth = w_in.shape[0]
    t = batch * seq
    tm = min(1024, t)
    tq = min(512, seq)
    te = 512
    tf = 512 if ffn_w_gate.shape[-1] % 512 == 0 else ffn_w_gate.shape[-1]

    xf = x.reshape(t, d)
    pos_f = positions.reshape(t, 1).astype(f32)
    inv_freq = ROPE_THETA ** (-jnp.arange(0, HEAD_DIM, 2, dtype=f32) / HEAD_DIM)
    inv_row = jnp.tile(inv_freq, LANES // (HEAD_DIM // 2)).reshape(1, LANES)
    cos, sin = rope_tables(pos_f, inv_row, tm)
    row = lambda v: v.reshape(1, -1)

    for i in range(depth):
        lambda_init = 0.8 - 0.6 * math.exp(-0.3 * i)
        proj = in_proj(xf, row(norm_g[i, 0]), w_in[i].astype(bf16), cos, sin, min(tm, 512), d)
        ya = rglru(proj, conv_w[i], row(conv_b[i]), _gate_weights(lru_wa[i], lru_wx[i]),
                   row(lru_ba[i]), row(lru_bx[i]), row(lru_lam[i]), batch, tq)
        yb = diff_attn(proj, diff_lambda[i], row(subln_g[i]), batch, tq, lambda_init)
        dense = i % 2 == 0
        xf, h = merge(ya, yb, proj, xf, w_proj_a[i].astype(bf16), w_proj_b[i].astype(bf16),
                      w_out[i].astype(bf16), row(norm_g[i, 1]), row(norm_g[i, 2]), tm, bf16 if dense else f32)
        tail = (row(norm_g[i, 3]), row(norm_g[i, 4]), p[i].reshape(t, -1), ple_w_gate[i].astype(bf16),
                ple_w_proj[i].astype(bf16), row(norm_g[i, 5]))
        if dense:
            xf = ffn(h, xf, ffn_w_gate[i // 2].astype(bf16), ffn_w_up[i // 2].astype(bf16),
                        ffn_w_down[i // 2].astype(bf16), tail, min(tm, 512), tf)
        else:
            xf = moe_routed(h, xf, router_w[i // 2].T.astype(bf16), moe_w_gate[i // 2].astype(bf16),
                        moe_w_up[i // 2].astype(bf16), moe_w_down[i // 2].astype(bf16), tail, tm, te, tf)
    return xf.reshape(batch, seq, d)
```

```python
import functools
import math

import jax
import jax.numpy as jnp
from jax import lax
from jax.experimental import pallas as pl
from jax.experimental.pallas import tpu as pltpu

EPS = 1e-6
CONV_WIDTH = 4
LRU_C = 8.0
N_RNN_BLOCKS = 16
N_HEADS = 8
HEAD_DIM = 64
V_DIM = 2 * HEAD_DIM
ROPE_THETA = 10000.0
TOP_K = 2
LANES = 128
SUBLANES = 8
MXU_DIM = 256
VMEM_LIMIT = 56 * 1024 * 1024
NEG_BIG = -1e30
ATTN_SCALE = HEAD_DIM ** -0.5 * math.log2(math.e)

f32 = jnp.float32
bf16 = jnp.bfloat16


def _cparams(*sem):
    return pltpu.CompilerParams(dimension_semantics=sem, vmem_limit_bytes=VMEM_LIMIT)


def _rms(x, g):
    return x * lax.rsqrt(jnp.mean(x * x, axis=-1, keepdims=True) + EPS) * g


def _sigmoid(x):
    return 1.0 / (1.0 + jnp.exp(-x))


def _sigmoid_tanh(x):
    return 0.5 * (1.0 + jnp.tanh(0.5 * x))


def _rope_tab_kernel(pos_ref, inv_ref, cos_ref, sin_ref):
    ang = pos_ref[...] * inv_ref[...]
    cos_ref[...] = jnp.cos(ang)
    sin_ref[...] = jnp.sin(ang)


def rope_tables(pos_f, inv_row, tm):
    t = pos_f.shape[0]
    return pl.pallas_call(
        _rope_tab_kernel,
        grid=(t // tm,),
        in_specs=[pl.BlockSpec((tm, 1), lambda i: (i, 0)),
                  pl.BlockSpec((1, LANES), lambda i: (0, 0))],
        out_specs=[pl.BlockSpec((tm, LANES), lambda i: (i, 0))] * 2,
        out_shape=[jax.ShapeDtypeStruct((t, LANES), f32)] * 2,
        compiler_params=_cparams("parallel"),
        name="rope_tables",
    )(pos_f, inv_row)


def _in_proj_kernel(x_ref, g_ref, w_ref, cos_ref, sin_ref, o_ref, *, tn, q_tile, k_tile):
    h = _rms(x_ref[...], g_ref[...]).astype(bf16)
    for j in range(w_ref.shape[1] // tn):
        cols = slice(j * tn, (j + 1) * tn)
        acc = jnp.dot(h, w_ref[:, cols], preferred_element_type=f32)
        if j not in (q_tile, k_tile):
            o_ref[:, cols] = acc.astype(o_ref.dtype)
            continue
        scale = ATTN_SCALE if j == q_tile else 1.0
        cos = cos_ref[...] * scale
        sin = sin_ref[...] * scale
        lane = lax.broadcasted_iota(jnp.int32, cos.shape, 1)
        first_half = (lane % HEAD_DIM) < (HEAD_DIM // 2)
        for g in range(tn // LANES):
            t = acc[:, g * LANES:(g + 1) * LANES]
            up = pltpu.roll(t, HEAD_DIM // 2, axis=1)
            down = pltpu.roll(t, LANES - HEAD_DIM // 2, axis=1)
            rot = jnp.where(first_half, -down, up)
            lanes = slice(j * tn + g * LANES, j * tn + (g + 1) * LANES)
            o_ref[:, lanes] = (t * cos + rot * sin).astype(o_ref.dtype)


def in_proj(x, g, w, cos, sin, tm, tn):
    t, d = x.shape
    n = w.shape[1]
    kern = functools.partial(_in_proj_kernel, tn=tn, q_tile=2 * d // tn, k_tile=3 * d // tn)
    return pl.pallas_call(
        kern,
        grid=(t // tm,),
        in_specs=[pl.BlockSpec((tm, d), lambda i: (i, 0)),
                  pl.BlockSpec((1, d), lambda i: (0, 0)),
                  pl.BlockSpec((d, n), lambda i: (0, 0), pipeline_mode=pl.Buffered(1)),
                  pl.BlockSpec((tm, LANES), lambda i: (i, 0)),
                  pl.BlockSpec((tm, LANES), lambda i: (i, 0))],
        out_specs=pl.BlockSpec((tm, n), lambda i: (i, 0)),
        out_shape=jax.ShapeDtypeStruct((t, n), bf16),
        compiler_params=_cparams("parallel"),
        name="in_proj",
    )(x, g, w, cos, sin)


def _rglru_kernel(gate_ref, x_ref, cw_ref, cb_ref, wbd_ref, ba_ref, bx_ref, lam_ref, o_ref,
                  xbuf, a_s, b_s, hcar, *, tc):
    c = pl.program_id(1)
    d = x_ref.shape[1]

    @pl.when(c == 0)
    def _():
        xbuf[0:SUBLANES, :] = jnp.zeros((SUBLANES, d), f32)
        hcar[...] = jnp.zeros_like(hcar)

    @pl.when(c > 0)
    def _():
        xbuf[0:SUBLANES, :] = xbuf[tc:tc + SUBLANES, :]

    xbuf[SUBLANES:SUBLANES + tc, :] = x_ref[...].astype(f32)

    u = cb_ref[...] + cw_ref[CONV_WIDTH - 1:CONV_WIDTH, :] * xbuf[SUBLANES:SUBLANES + tc, :]
    for k in range(CONV_WIDTH - 1):
        off = SUBLANES - (CONV_WIDTH - 1) + k
        u = u + cw_ref[k:k + 1, :] * xbuf[off:off + tc, :]

    ub = u.astype(bf16)
    z = -lam_ref[...]
    softplus = jnp.maximum(z, 0.0) + jnp.log(1.0 + jnp.exp(-jnp.abs(z)))
    for g in range(d // MXU_DIM):
        sl = slice(g * MXU_DIM, (g + 1) * MXU_DIM)
        zz = jnp.dot(ub[:, sl], wbd_ref[g], preferred_element_type=f32)
        r = _sigmoid_tanh(zz[:, :MXU_DIM] + ba_ref[:, sl])
        i_gate = _sigmoid_tanh(zz[:, MXU_DIM:] + bx_ref[:, sl])
        log_a = (-LRU_C) * r * softplus[:, sl]
        a = jnp.exp(log_a)
        beta = jnp.sqrt(1.0 - a * a)
        a_s[:, sl] = a
        b_s[:, sl] = beta * (i_gate * u[:, sl])

    row = lax.broadcasted_iota(jnp.int32, (SUBLANES, d), 0)

    def body(i, h):
        rows = pl.ds(pl.multiple_of(i * SUBLANES, SUBLANES), SUBLANES)
        a = a_s[rows, :]
        b = b_s[rows, :]
        shift = 1
        while shift < SUBLANES:
            keep = row >= shift
            b = a * jnp.where(keep, pltpu.roll(b, shift, axis=0), 0.0) + b
            a = a * jnp.where(keep, pltpu.roll(a, shift, axis=0), 1.0)
            shift *= 2
        h_tile = a * h + b
        b_s[rows, :] = h_tile
        return h_tile[SUBLANES - 1:SUBLANES, :]

    hcar[...] = lax.fori_loop(0, tc // SUBLANES, body, hcar[...], unroll=2)

    gt = gate_ref[...].astype(f32)
    gelu = 0.5 * gt * (1.0 + jnp.tanh(math.sqrt(2.0 / math.pi) * (gt + 0.044715 * (gt * gt * gt))))
    o_ref[...] = (b_s[...] * gelu).astype(o_ref.dtype)


def rglru(proj, cw, cb, wbd, ba, bx, lam, batch, tc):
    t = proj.shape[0]
    d = cw.shape[1]
    nc = t // batch // tc
    row = lambda b, c: (b * nc + c, 0)
    const2 = lambda b, c: (0, 0)
    return pl.pallas_call(
        functools.partial(_rglru_kernel, tc=tc),
        grid=(batch, nc),
        in_specs=[pl.BlockSpec((tc, d), lambda b, c: (b * nc + c, 0)),
                  pl.BlockSpec((tc, d), lambda b, c: (b * nc + c, 1)),
                  pl.BlockSpec((CONV_WIDTH, d), const2),
                  pl.BlockSpec((1, d), const2),
                  pl.BlockSpec(wbd.shape, lambda b, c: (0, 0, 0)),
                  pl.BlockSpec((1, d), const2),
                  pl.BlockSpec((1, d), const2),
                  pl.BlockSpec((1, d), const2)],
        out_specs=pl.BlockSpec((tc, d), row),
        out_shape=jax.ShapeDtypeStruct((t, d), bf16),
        scratch_shapes=[pltpu.VMEM((tc + SUBLANES, d), f32),
                        pltpu.VMEM((tc, d), f32),
                        pltpu.VMEM((tc, d), f32),
                        pltpu.VMEM((1, d), f32)],
        compiler_params=_cparams("parallel", "arbitrary"),
        name="rglru",
    )(proj, proj, cw, cb, wbd, ba, bx, lam)


def _diff_attn_kernel(q_ref, k_ref, v_ref, dl_ref, gs_ref, o_ref, vt, qt, s_a, s_b, m_s, l_s, acc,
                      *, tq, lambda_init):
    n_kv = k_ref.shape[0] // tq
    for c in range(n_kv):
        vt[c] = v_ref[c * tq:(c + 1) * tq, :].astype(f32).T.astype(bf16)
    dl = dl_ref[...]
    lam = (jnp.exp(jnp.sum(dl[0:1, :] * dl[1:2, :], axis=-1, keepdims=True))
           - jnp.exp(jnp.sum(dl[2:3, :] * dl[3:4, :], axis=-1, keepdims=True)) + lambda_init)

    def rows_of(tile):
        return slice(tile * tq, (tile + 1) * tq)

    def scores(item, s_ref):
        w, j, _ = item
        s_ref[...] = jnp.dot(k_ref[rows_of(j), :], qt[w], preferred_element_type=f32)

    def consume(item, s_ref):
        w, j, masked = item
        s_t = s_ref[...]
        if masked:
            key = lax.broadcasted_iota(jnp.int32, s_t.shape, 0)
            query = lax.broadcasted_iota(jnp.int32, s_t.shape, 1) % tq
            s_t = jnp.where(query >= key, s_t, NEG_BIG)
        m_old = m_s[w]
        m_new = jnp.maximum(m_old, jnp.max(s_t, axis=0, keepdims=True))
        alpha = jnp.exp2(m_old - m_new)
        p_t = jnp.exp2(s_t - m_new)
        l_s[w] = alpha * l_s[w] + jnp.sum(p_t, axis=0, keepdims=True)
        acc[w] = alpha * acc[w] + jnp.dot(vt[j], p_t.astype(bf16), preferred_element_type=f32)
        m_s[w] = m_new

    def finalize(w, q_tile):
        o_all = acc[w] * (1.0 / l_s[w])
        o_t = o_all[:, 0:tq] - lam * o_all[:, tq:2 * tq]
        o_t = o_t * lax.rsqrt(jnp.mean(o_t * o_t, axis=0, keepdims=True) + EPS)
        o_ref[rows_of(q_tile), :] = (o_t.T * gs_ref[...] * (1.0 - lambda_init)).astype(o_ref.dtype)

    items = [(q, j, j == q) for q in range(n_kv) for j in range(q + 1)]
    bufs = (s_a, s_b)

    def start_query_tile(q):
        w = q % 2
        q_t = q_ref[rows_of(q), :].astype(f32).T
        row = lax.broadcasted_iota(jnp.int32, q_t.shape, 0)
        qt[w, :, 0:tq] = jnp.where(row < HEAD_DIM, q_t, 0.0).astype(bf16)
        qt[w, :, tq:2 * tq] = jnp.where(row >= HEAD_DIM, q_t, 0.0).astype(bf16)
        m_s[w] = jnp.full(m_s.shape[1:], NEG_BIG, f32)
        l_s[w] = jnp.zeros(l_s.shape[1:], f32)
        acc[w] = jnp.zeros(acc.shape[1:], f32)

    start_query_tile(0)
    scores((0, 0, True), bufs[0])
    for i, (q, j, masked) in enumerate(items):
        if i + 1 < len(items):
            qn, jn, mn = items[i + 1]
            if jn == 0:
                start_query_tile(qn)
            scores((qn % 2, jn, mn), bufs[(i + 1) % 2])
        consume((q % 2, j, masked), bufs[i % 2])
        if masked:
            finalize(q % 2, q)


def diff_attn(proj, dl, gs, batch, tq, lambda_init):
    t = proj.shape[0]
    seq = t // batch
    nq = seq // tq
    d = N_HEADS * V_DIM
    q_col, k_col, v_col = 2 * N_HEADS, 3 * N_HEADS, 4 * N_HEADS
    kern = functools.partial(_diff_attn_kernel, tq=tq, lambda_init=lambda_init)
    return pl.pallas_call(
        kern,
        grid=(batch, N_HEADS),
        in_specs=[pl.BlockSpec((seq, V_DIM), lambda b, h: (b, q_col + h)),
                  pl.BlockSpec((seq, V_DIM), lambda b, h: (b, k_col + h)),
                  pl.BlockSpec((seq, V_DIM), lambda b, h: (b, v_col + h)),
                  pl.BlockSpec((4, HEAD_DIM), lambda b, h: (0, 0)),
                  pl.BlockSpec((1, V_DIM), lambda b, h: (0, 0))],
        out_specs=pl.BlockSpec((seq, V_DIM), lambda b, h: (b, h)),
        out_shape=jax.ShapeDtypeStruct((t, d), bf16),
        scratch_shapes=[pltpu.VMEM((nq, V_DIM, tq), bf16),
                        pltpu.VMEM((2, V_DIM, 2 * tq), bf16),
                        pltpu.VMEM((tq, 2 * tq), f32),
                        pltpu.VMEM((tq, 2 * tq), f32),
                        pltpu.VMEM((2, 1, 2 * tq), f32),
                        pltpu.VMEM((2, 1, 2 * tq), f32),
                        pltpu.VMEM((2, V_DIM, 2 * tq), f32)],
        compiler_params=_cparams("parallel", "parallel"),
        name="diff_attn",
    )(proj, proj, proj, dl, gs)


def _merge_kernel(ya_ref, yb_ref, ga_ref, gb_ref, x_ref, wa_ref, wb_ref, wo_ref, g1_ref, g2_ref,
                  xo_ref, ho_ref):
    pa = jnp.dot(ya_ref[...], wa_ref[...], preferred_element_type=f32)
    pb = jnp.dot(yb_ref[...], wb_ref[...], preferred_element_type=f32)
    merged = _sigmoid(ga_ref[...].astype(f32)) * pa + _sigmoid(gb_ref[...].astype(f32)) * pb
    o = jnp.dot(merged.astype(bf16), wo_ref[...], preferred_element_type=f32)
    x_new = x_ref[...] + _rms(o, g1_ref[...])
    xo_ref[...] = x_new
    ho_ref[...] = _rms(x_new, g2_ref[...]).astype(ho_ref.dtype)


def merge(ya, yb, proj, x, wa, wb, wo, g1, g2, tm, h_dtype):
    t, d = x.shape
    ga_col = proj.shape[1] // d - 2
    row = lambda i: (i, 0)
    const = lambda i: (0, 0)
    return pl.pallas_call(
        _merge_kernel,
        grid=(t // tm,),
        in_specs=[pl.BlockSpec((tm, d), row), pl.BlockSpec((tm, d), row),
                  pl.BlockSpec((tm, d), lambda i: (i, ga_col)),
                  pl.BlockSpec((tm, d), lambda i: (i, ga_col + 1)),
                  pl.BlockSpec((tm, d), row),
                  pl.BlockSpec((d, d), const), pl.BlockSpec((d, d), const), pl.BlockSpec((d, d), const),
                  pl.BlockSpec((1, d), const), pl.BlockSpec((1, d), const)],
        out_specs=[pl.BlockSpec((tm, d), row), pl.BlockSpec((tm, d), row)],
        out_shape=[jax.ShapeDtypeStruct((t, d), f32), jax.ShapeDtypeStruct((t, d), h_dtype)],
        compiler_params=_cparams("parallel"),
        name="merge",
    )(ya, yb, proj, proj, x, wa, wb, wo, g1, g2)


def _mixer_tail(x, f, g3_ref, g4_ref, p_ref, pwg_ref, pwp_ref, g5_ref, xo_ref):
    x = x + _rms(f, g3_ref[...])
    gate = _sigmoid(jnp.dot(_rms(x, g4_ref[...]).astype(bf16), pwg_ref[...], preferred_element_type=f32))
    emb = jnp.dot(p_ref[...].astype(bf16), pwp_ref[...], preferred_element_type=f32)
    xo_ref[...] = x + _rms(gate * emb, g5_ref[...])


def _tail_specs(tm, d, dp):
    row = lambda i: (i, 0)
    const = lambda i: (0, 0)
    vec = pl.BlockSpec((1, d), const)
    once = pl.Buffered(1)
    in_specs = [vec, vec, pl.BlockSpec((tm, dp), row),
                pl.BlockSpec((d, d), const, pipeline_mode=once),
                pl.BlockSpec((dp, d), const, pipeline_mode=once), vec]
    return in_specs, pl.BlockSpec((tm, d), row)


def _swiglu(h, wg_ref, wu_ref, wd_ref, act, tf):
    for c in range(act.shape[1] // tf):
        sl = slice(c * tf, (c + 1) * tf)
        gate = jnp.dot(h, wg_ref[:, sl], preferred_element_type=f32)
        up = jnp.dot(h, wu_ref[:, sl], preferred_element_type=f32)
        act[:, sl] = (gate * _sigmoid(gate) * up).astype(bf16)
    return jnp.dot(act[...], wd_ref[...], preferred_element_type=f32)


def _ffn_kernel(h_ref, x_ref, wg_ref, wu_ref, wd_ref, *rest, tf):
    *tail, act = rest
    _mixer_tail(x_ref[...], _swiglu(h_ref[...], wg_ref, wu_ref, wd_ref, act, tf), *tail)


def ffn(h, x, wg, wu, wd, tail, tm, tf):
    t, d = x.shape
    dff = wg.shape[1]
    row = lambda i: (i, 0)
    const = lambda i: (0, 0)
    once = pl.Buffered(1)
    tail_in, outs = _tail_specs(tm, d, tail[2].shape[1])
    return pl.pallas_call(
        functools.partial(_ffn_kernel, tf=tf),
        grid=(t // tm,),
        in_specs=[pl.BlockSpec((tm, d), row), pl.BlockSpec((tm, d), row),
                  pl.BlockSpec((d, dff), const, pipeline_mode=once),
                  pl.BlockSpec((d, dff), const, pipeline_mode=once),
                  pl.BlockSpec((dff, d), const, pipeline_mode=once)] + tail_in,
        out_specs=outs,
        out_shape=jax.ShapeDtypeStruct((t, d), f32),
        scratch_shapes=[pltpu.VMEM((tm, dff), bf16)],
        compiler_params=_cparams("parallel"),
        name="ffn",
    )(h, x, wg, wu, wd, *tail)


def _router_kernel(h_ref, rw_ref, idx_ref, w_ref, cnt_ref, tri, cnt):
    tm = h_ref.shape[0]

    @pl.when(pl.program_id(0) == 0)
    def _():
        before = lax.broadcasted_iota(jnp.int32, (tm, tm), 0) < lax.broadcasted_iota(jnp.int32, (tm, tm), 1)
        tri[...] = jnp.where(before, 1.0, 0.0).astype(bf16)
        cnt[...] = jnp.zeros_like(cnt)

    logits = lax.dot_general(rw_ref[...], h_ref[...].astype(bf16), (((1,), (1,)), ((), ())),
                             preferred_element_type=f32)
    e_idx = lax.broadcasted_iota(jnp.int32, logits.shape, 0)
    n_e = logits.shape[0]
    v1 = jnp.max(logits, axis=0, keepdims=True)
    i1 = jnp.min(jnp.where(logits == v1, e_idx, n_e), axis=0, keepdims=True)
    rest = jnp.where(e_idx == i1, -jnp.inf, logits)
    v2 = jnp.max(rest, axis=0, keepdims=True)
    i2 = jnp.min(jnp.where(rest == v2, e_idx, n_e), axis=0, keepdims=True)
    ex = jnp.exp(v2 - v1)
    w_ref[0:1, :] = 1.0 / (1.0 + ex)
    w_ref[1:2, :] = ex / (1.0 + ex)

    sel1 = e_idx == i1
    sel2 = e_idx == i2
    sel = jnp.where(jnp.logical_or(sel1, sel2), 1.0, 0.0)
    prefix = jnp.dot(sel.astype(bf16), tri[...], preferred_element_type=f32)
    rank = prefix + cnt[:, 0:1]
    idx_ref[0:1, :] = i1
    idx_ref[1:2, :] = i2
    idx_ref[2:3, :] = jnp.sum(jnp.where(sel1, rank, 0.0), axis=0, keepdims=True).astype(jnp.int32)
    idx_ref[3:4, :] = jnp.sum(jnp.where(sel2, rank, 0.0), axis=0, keepdims=True).astype(jnp.int32)
    cnt[...] = cnt[...] + jnp.sum(sel, axis=1, keepdims=True)
    cnt_ref[...] = cnt[...]


def router(h, rw_t, tm):
    t, d = h.shape
    n_e = rw_t.shape[0]
    return pl.pallas_call(
        _router_kernel,
        grid=(t // tm,),
        in_specs=[pl.BlockSpec((tm, d), lambda i: (i, 0)),
                  pl.BlockSpec((n_e, d), lambda i: (0, 0))],
        out_specs=[pl.BlockSpec((4, tm), lambda i: (0, i)),
                   pl.BlockSpec((2, tm), lambda i: (0, i)),
                   pl.BlockSpec((n_e, LANES), lambda i: (0, 0))],
        out_shape=[jax.ShapeDtypeStruct((4, t), jnp.int32),
                   jax.ShapeDtypeStruct((2, t), f32),
                   jax.ShapeDtypeStruct((n_e, LANES), f32)],
        scratch_shapes=[pltpu.VMEM((tm, tm), bf16), pltpu.VMEM((n_e, LANES), f32)],
        compiler_params=_cparams("arbitrary"),
        name="router",
    )(h, rw_t)


def _dispatch_kernel(slot1_ref, slot2_ref, ends_ref, h_ref, xs_hbm, zeros, sem, *, te):
    tc = h_ref.shape[0]

    @pl.when(pl.program_id(0) == 0)
    def _():
        zeros[...] = jnp.zeros_like(zeros)
        def clear(start):
            copy = pltpu.make_async_copy(zeros, xs_hbm.at[pl.ds(pl.multiple_of(start, te), te)], sem)
            copy.start()
            copy.wait()

        n_e = ends_ref.shape[0]
        for e in range(n_e):
            clear(jnp.maximum(ends_ref[e] - te, 0))
        for k in range(n_e):
            start = ends_ref[n_e - 1] + k * te
            pl.when(start < xs_hbm.shape[0])(functools.partial(clear, start))

    def body(t, carry):
        for slot_ref in (slot1_ref, slot2_ref):
            pltpu.make_async_copy(h_ref.at[pl.ds(t, 1)], xs_hbm.at[pl.ds(slot_ref[t], 1)], sem).start()
        return carry

    lax.fori_loop(0, tc, body, 0, unroll=8)
    for _ in range(TOP_K):
        pltpu.make_async_copy(h_ref, xs_hbm.at[pl.ds(0, tc)], sem).wait()


def dispatch(slots, ends, h, n_slots, tc, te):
    t, d = h.shape
    smem_1d = pl.BlockSpec((tc,), lambda i: (i,), memory_space=pltpu.SMEM)
    return pl.pallas_call(
        functools.partial(_dispatch_kernel, te=te),
        grid=(t // tc,),
        in_specs=[smem_1d, smem_1d,
                  pl.BlockSpec(ends.shape, lambda i: (0,), memory_space=pltpu.SMEM),
                  pl.BlockSpec((tc, d), lambda i: (i, 0))],
        out_specs=pl.BlockSpec(memory_space=pl.ANY),
        out_shape=jax.ShapeDtypeStruct((n_slots, d), h.dtype),
        scratch_shapes=[pltpu.VMEM((te, d), h.dtype), pltpu.SemaphoreType.DMA(())],
        compiler_params=_cparams("arbitrary"),
        name="dispatch",
    )(slots[0], slots[1], ends, h)


def _experts_kernel(te_ref, rt_ref, nu_ref, xs_ref, wg_ref, wu_ref, wd_ref, ys_ref, act, *, tf):
    del te_ref, rt_ref

    @pl.when(pl.program_id(0) < nu_ref[0])
    def _():
        ys_ref[...] = _swiglu(xs_ref[...].astype(bf16), wg_ref.at[0], wu_ref.at[0], wd_ref.at[0], act, tf)

    @pl.when(pl.program_id(0) >= nu_ref[0])
    def _():
        ys_ref[...] = jnp.zeros_like(ys_ref)


def experts(tile_expert, row_tile, n_used, xs, wg, wu, wd, te, tf):
    n_slots, d = xs.shape
    dff = wg.shape[2]
    once = pl.Buffered(1)
    return pl.pallas_call(
        functools.partial(_experts_kernel, tf=tf),
        grid_spec=pltpu.PrefetchScalarGridSpec(
            num_scalar_prefetch=3,
            grid=(n_slots // te,),
            in_specs=[pl.BlockSpec((te, d), lambda i, et, rt, nu: (rt[i], 0)),
                      pl.BlockSpec((1, d, dff), lambda i, et, rt, nu: (et[i], 0, 0)),
                      pl.BlockSpec((1, d, dff), lambda i, et, rt, nu: (et[i], 0, 0)),
                      pl.BlockSpec((1, dff, d), lambda i, et, rt, nu: (et[i], 0, 0), pipeline_mode=once)],
            out_specs=pl.BlockSpec((te, d), lambda i, et, rt, nu: (i, 0)),
            scratch_shapes=[pltpu.VMEM((te, dff), bf16)]),
        out_shape=jax.ShapeDtypeStruct((n_slots, d), f32),
        compiler_params=_cparams("arbitrary"),
        name="experts",
    )(tile_expert, row_tile, n_used, xs, wg, wu, wd)


def _combine_kernel(slot1_ref, slot2_ref, next1_ref, next2_ref, ys_hbm, w_ref, x_ref, *rest):
    *tail, buf, sem = rest
    tm = x_ref.shape[0]
    i = pl.program_id(0)
    cur = i % 2
    nxt = 1 - cur

    def row_copy(slot_refs, b, t):
        return [pltpu.make_async_copy(ys_hbm.at[pl.ds(slot_ref[t], 1)], buf.at[b, k, pl.ds(t, 1)], sem.at[b])
                for k, slot_ref in enumerate(slot_refs)]

    def wait_rows(b):
        for k in range(TOP_K):
            pltpu.make_async_copy(ys_hbm.at[pl.ds(0, tm)], buf.at[b, k], sem.at[b]).wait()

    @pl.when(i == 0)
    def _():
        def body(t, carry):
            for copy in row_copy((slot1_ref, slot2_ref), 0, t):
                copy.start()
            return carry

        lax.fori_loop(0, tm, body, 0, unroll=8)

    wait_rows(cur)
    for t in range(tm):
        for copy in row_copy((next1_ref, next2_ref), nxt, t):
            copy.start()

    w = w_ref[...]
    _mixer_tail(x_ref[...], w[:, 0:1] * buf[cur, 0] + w[:, 1:2] * buf[cur, 1], *tail)

    @pl.when(i == pl.num_programs(0) - 1)
    def _():
        wait_rows(nxt)


def combine(slots, ys, w_cols, x, tail, tm):
    t, d = x.shape
    n = t // tm
    row = lambda i: (i, 0)
    smem_cur = pl.BlockSpec((tm,), lambda i: (i,), memory_space=pltpu.SMEM)
    smem_next = pl.BlockSpec((tm,), lambda i: (jnp.minimum(i + 1, n - 1),), memory_space=pltpu.SMEM)
    tail_in, outs = _tail_specs(tm, d, tail[2].shape[1])
    return pl.pallas_call(
        _combine_kernel,
        grid=(n,),
        in_specs=[smem_cur, smem_cur, smem_next, smem_next,
                  pl.BlockSpec(memory_space=pl.ANY),
                  pl.BlockSpec((tm, TOP_K), row), pl.BlockSpec((tm, d), row)] + tail_in,
        out_specs=outs,
        out_shape=jax.ShapeDtypeStruct((t, d), f32),
        scratch_shapes=[pltpu.VMEM((2, TOP_K, tm, d), f32), pltpu.SemaphoreType.DMA((2,))],
        compiler_params=_cparams("arbitrary"),
        name="combine",
    )(slots[0], slots[1], slots[0], slots[1], ys, w_cols, x, *tail)


def moe_routed(h, x, rw_t, wg, wu, wd, tail, tm, te, tf):
    t, d = h.shape
    n_e = wg.shape[0]
    idx, w_rows, cnt = router(h, rw_t, tm)
    counts = cnt[:, 0].astype(jnp.int32)
    padded = (counts + te - 1) // te * te
    ends = jnp.cumsum(padded)
    offs = ends - padded
    slots = jnp.stack([offs[idx[0]] + idx[2], offs[idx[1]] + idx[3]])
    n_tiles = TOP_K * t // te + n_e
    n_used = ends[-1] // te
    tiles = jnp.minimum(jnp.arange(n_tiles, dtype=jnp.int32), n_used - 1)
    tile_expert = jnp.sum((tiles[:, None] >= (ends // te)[None, :]).astype(jnp.int32), axis=1)
    xs = dispatch(slots, ends.astype(jnp.int32), h, n_tiles * te, tm, te)
    ys = experts(tile_expert, tiles, n_used.reshape(1).astype(jnp.int32), xs, wg, wu, wd, te, tf)
    return combine(slots, ys, w_rows.T, x, tail, min(tm, 512))


def _gate_weights(wa, wx):
    per = MXU_DIM // wa.shape[1]

    def tiles(w):
        groups = w.reshape(w.shape[0] // per, per, w.shape[1], w.shape[2])
        eye = jnp.eye(per, dtype=w.dtype)
        return jnp.einsum('gpcd,pq->gpcqd', groups, eye).reshape(groups.shape[0], MXU_DIM, MXU_DIM)

    return jnp.concatenate([tiles(wa), tiles(wx)], axis=-1).astype(bf16)


def kernel(x, p, positions, norm_g, w_in, conv_w, conv_b, lru_wa, lru_ba, lru_wx, lru_bx, lru_lam,
           diff_lambda, subln_g, w_proj_a, w_proj_b, w_out, ffn_w_gate, ffn_w_up, ffn_w_down, router_w,
           moe_w_gate, moe_w_up, moe_w_down, ple_w_proj, ple_w_gate):
    batch, seq, d = x.shape
    depth = w_in.shape[0]
    t = batch * seq
    tm = min(1024, t)
    tq = min(512, seq)
    te = 512
    tf = 512 if ffn_w_gate.shape[-1] % 512 == 0 else ffn_w_gate.shape[-1]

    xf = x.reshape(t, d)
    pos_f = positions.reshape(t, 1).astype(f32)
    inv_freq = ROPE_THETA ** (-jnp.arange(0, HEAD_DIM, 2, dtype=f32) / HEAD_DIM)
    inv_row = jnp.tile(inv_freq, LANES // (HEAD_DIM // 2)).reshape(1, LANES)
    cos, sin = rope_tables(pos_f, inv_row, tm)
    row = lambda v: v.reshape(1, -1)

    for i in range(depth):
        lambda_init = 0.8 - 0.6 * math.exp(-0.3 * i)
        proj = in_proj(xf, row(norm_g[i, 0]), w_in[i].astype(bf16), cos, sin, min(tm, 512), d)
        ya = rglru(proj, conv_w[i], row(conv_b[i]), _gate_weights(lru_wa[i], lru_wx[i]),
                   row(lru_ba[i]), row(lru_bx[i]), row(lru_lam[i]), batch, tq)
        yb = diff_attn(proj, diff_lambda[i], row(subln_g[i]), batch, tq, lambda_init)
        dense = i % 2 == 0
        xf, h = merge(ya, yb, proj, xf, w_proj_a[i].astype(bf16), w_proj_b[i].astype(bf16),
                      w_out[i].astype(bf16), row(norm_g[i, 1]), row(norm_g[i, 2]), tm, bf16 if dense else f32)
        tail = (row(norm_g[i, 3]), row(norm_g[i, 4]), p[i].reshape(t, -1), ple_w_gate[i].astype(bf16),
                ple_w_proj[i].astype(bf16), row(norm_g[i, 5]))
        if dense:
            xf = ffn(h, xf, ffn_w_gate[i // 2].astype(bf16), ffn_w_up[i // 2].astype(bf16),
                        ffn_w_down[i // 2].astype(bf16), tail, min(tm, 512), tf)
        else:
            xf = moe_routed(h, xf, router_w[i // 2].T.astype(bf16), moe_w_gate[i // 2].astype(bf16),
                        moe_w_up[i // 2].astype(bf16), moe_w_down[i // 2].astype(bf16), tail, tm, te, tf)
    return xf.reshape(batch, seq, d)
```
